```python
import math
import jax, jax.numpy as jnp
from jax import lax
import numpy as np

D_MODEL = 1024
BATCH = 8
SEQ = 4096
DEPTH = 4

GRID_W = 64
HEAD_DIM = 64
ATT_HEADS = 8
ATT_KV_HEADS = 2
GQA_GROUP = ATT_HEADS // ATT_KV_HEADS
ATT_WIDTH = ATT_HEADS * HEAD_DIM
ATT_KV_WIDTH = ATT_KV_HEADS * HEAD_DIM
Q_BLOCK = 128
ROPE_THETA = 10000.0
ROPE_FREQS = HEAD_DIM // 4
GLA_HEADS = 4
GLA_DK = 32
GLA_DV = 64
GLA_QK_WIDTH = GLA_HEADS * GLA_DK
GLA_WIDTH = GLA_HEADS * GLA_DV
GLA_GATE_RANK = 16
GLA_TAU = 16.0
GLA_CHUNK = 64
FNET_GROUPS = 4
FNET_GROUP_DIM = 64
FNET_WIDTH = FNET_GROUPS * FNET_GROUP_DIM
MIX_WIDTH = ATT_WIDTH + GLA_WIDTH + FNET_WIDTH
IN_SPLITS = (ATT_WIDTH, ATT_KV_WIDTH, ATT_KV_WIDTH, GLA_QK_WIDTH, GLA_QK_WIDTH,
             GLA_WIDTH, GLA_WIDTH, 2 * GLA_GATE_RANK, FNET_WIDTH)
IN_WIDTH = 1824
N_EXPERTS = 32
TOP_K = 4
D_EXPERT = 1024
SWIGLU_LIMIT = 7.0
SWIGLU_ALPHA = 1.702
LN_EPS = 1e-5
RMS_EPS = 1e-6
DN_ALPHA = (2.0 * DEPTH) ** 0.25
DN_BETA = (8.0 * DEPTH) ** -0.25

kernel_name = "hymba_style_gla_gqa_fnet_moe_deepnorm_encoder"


def layer_norm(x, g, b):
    xf = x.astype(jnp.float32)
    mu = jnp.mean(xf, axis=-1, keepdims=True)
    var = jnp.mean(jnp.square(xf - mu), axis=-1, keepdims=True)
    return ((xf - mu) * lax.rsqrt(var + LN_EPS) * g + b).astype(x.dtype)


def rms_norm_f32(x, g):
    xf = x.astype(jnp.float32)
    return xf * lax.rsqrt(jnp.mean(jnp.square(xf), axis=-1, keepdims=True) + RMS_EPS) * g.astype(jnp.float32)


def split_columns(proj):
    points, acc = [], 0
    for w in IN_SPLITS[:-1]:
        acc += w
        points.append(acc)
    return jnp.split(proj, points, axis=-1)


def axial_rope_tables(seq_len):
    rows = seq_len // GRID_W
    row_id = jnp.repeat(jnp.arange(rows, dtype=jnp.float32), GRID_W, total_repeat_length=seq_len)
    col_id = (jnp.arange(seq_len) % GRID_W).astype(jnp.float32)
    inv_freq = 1.0 / (ROPE_THETA ** (jnp.arange(ROPE_FREQS, dtype=jnp.float32) / ROPE_FREQS))
    ang = jnp.stack([row_id[:, None] * inv_freq, col_id[:, None] * inv_freq], axis=1)
    return jnp.cos(ang), jnp.sin(ang)


def apply_axial_rope(x, cos, sin):
    xr = x.reshape(*x.shape[:-1], 2, 2, ROPE_FREQS)
    x1, x2 = xr[..., 0, :], xr[..., 1, :]
    out = jnp.stack([x1 * cos - x2 * sin, x1 * sin + x2 * cos], axis=-2)
    return out.reshape(x.shape)


def gqa_axial_attention(q, k, v, q_gain, k_gain, cos, sin):
    b, s, _ = q.shape
    qh = q.reshape(b, s, ATT_KV_HEADS, GQA_GROUP, HEAD_DIM).transpose(0, 2, 3, 1, 4)
    kh = k.reshape(b, s, ATT_KV_HEADS, HEAD_DIM).transpose(0, 2, 1, 3)
    vh = v.reshape(b, s, ATT_KV_HEADS, HEAD_DIM).transpose(0, 2, 1, 3).astype(jnp.float32)
    qh = apply_axial_rope(rms_norm_f32(qh, q_gain), cos, sin) * (HEAD_DIM ** -0.5)
    kh = apply_axial_rope(rms_norm_f32(kh, k_gain), cos, sin)
    nb = s // Q_BLOCK
    qb = qh.reshape(b, ATT_KV_HEADS, GQA_GROUP, nb, Q_BLOCK, HEAD_DIM).transpose(3, 0, 1, 2, 4, 5)

    def one_block(q_blk):
        scores = jnp.einsum('bkgqd,bksd->bkgqs', q_blk, kh)
        p = jax.nn.softmax(scores, axis=-1)
        return jnp.einsum('bkgqs,bksd->bkgqd', p, vh)

    o = lax.map(one_block, qb)
    return o.transpose(1, 0, 4, 2, 3, 5).reshape(b, s, ATT_WIDTH)


def gla_chunked(q, k, v, log_a, strict):
    b, h, s, dk = q.shape
    dv = v.shape[-1]
    nc = s // GLA_CHUNK

    def to_chunks(t):
        return t.reshape(b, h, nc, GLA_CHUNK, t.shape[-1]).transpose(2, 0, 1, 3, 4)

    idx = jnp.arange(GLA_CHUNK)
    mask = (idx[:, None] > idx[None, :]) if strict else (idx[:, None] >= idx[None, :])

    def step(state, inp):
        qc, kc, vc, ac = inp
        bc = jnp.cumsum(ac, axis=-2)
        diff = bc[..., :, None, :] - bc[..., None, :, :]
        decay = jnp.exp(jnp.where(mask[:, :, None], diff, -jnp.inf))
        scores = jnp.einsum('bhid,bhjd,bhijd->bhij', qc, kc, decay)
        o = (jnp.einsum('bhij,bhjv->bhiv', scores, vc)
             + jnp.einsum('bhid,bhdv->bhiv', qc * jnp.exp(bc), state))
        b_last = bc[..., -1, :]
        k_dec = kc * jnp.exp(b_last[..., None, :] - bc)
        state = state * jnp.exp(b_last)[..., None] + jnp.einsum('bhjd,bhjv->bhdv', k_dec, vc)
        return state, o

    state0 = jnp.zeros((b, h, dk, dv), jnp.float32)
    _, o = lax.scan(step, state0, (to_chunks(q), to_chunks(k), to_chunks(v), to_chunks(log_a)))
    return o.transpose(1, 2, 0, 3, 4).reshape(b, h, s, dv)


def gla_mixer(q, k, v, r, lowrank, gate_w, gate_b, norm_g):
    b, s, _ = q.shape

    def heads(t, d):
        return t.reshape(b, s, GLA_HEADS, d).transpose(0, 2, 1, 3).astype(jnp.float32)

    qh = heads(q, GLA_DK) * (GLA_DK ** -0.5)
    kh = heads(k, GLA_DK)
    vh = heads(v, GLA_DV)
    z = jnp.einsum('bsnr,nrk->bsnk', lowrank.reshape(b, s, 2, GLA_GATE_RANK), gate_w) + gate_b
    log_a = jax.nn.log_sigmoid(z.astype(jnp.float32)) / GLA_TAU
    la_fwd = heads(log_a[:, :, 0], GLA_DK)
    la_bwd = heads(log_a[:, :, 1], GLA_DK)
    o_fwd = gla_chunked(qh, kh, vh, la_fwd, strict=False)
    flip = lambda t: jnp.flip(t, axis=2)
    o_bwd = flip(gla_chunked(flip(qh), flip(kh), flip(vh), flip(la_bwd), strict=True))
    o = rms_norm_f32(o_fwd + o_bwd, norm_g)
    o = o.transpose(0, 2, 1, 3).reshape(b, s, GLA_WIDTH)
    return o * jax.nn.silu(r.astype(jnp.float32))


def fourier_mixer(f):
    b, s, _ = f.shape
    fg = f.reshape(b, s, FNET_GROUPS, FNET_GROUP_DIM).astype(jnp.float32)
    return jnp.fft.fftn(fg, axes=(1, 3), norm='ortho').real.reshape(b, s, FNET_WIDTH)


def moe_ffn(x, router_w, router_b, w_gu, b_gu, w_down, b_down):
    b, s, d = x.shape
    xt = x.reshape(b * s, d)
    logits = (xt @ router_w + router_b).astype(jnp.float32)
    top_val, top_idx = lax.top_k(logits, TOP_K)
    top_w = jax.nn.softmax(top_val, axis=-1)
    combine = jnp.sum(jax.nn.one_hot(top_idx, N_EXPERTS, dtype=jnp.float32) * top_w[..., None], axis=1)
    y = jnp.zeros((b * s, d), jnp.float32)
    for e in range(N_EXPERTS):
        gu = xt @ w_gu[e] + b_gu[e]
        gate, up = jnp.split(gu, 2, axis=-1)
        gate = jnp.minimum(gate, SWIGLU_LIMIT)
        up = jnp.clip(up, -SWIGLU_LIMIT, SWIGLU_LIMIT)
        hid = (up + 1.0) * (gate * jax.nn.sigmoid(SWIGLU_ALPHA * gate))
        y = y + combine[:, e:e + 1] * (hid @ w_down[e] + b_down[e]).astype(jnp.float32)
    return y.reshape(b, s, d).astype(x.dtype)


def setup_inputs(seed: int = 0) -> dict:
    key = jax.random.key(seed)
    ks = jax.random.split(key, 20)
    f32 = jnp.float32
    nrm = lambda k, shape, scale: jax.random.normal(k, shape, f32) * scale
    L = DEPTH
    return {
        "x": nrm(ks[0], (BATCH, SEQ, D_MODEL), 1.0),
        "ln_in_g": 1.0 + nrm(ks[1], (D_MODEL,), 0.02),
        "ln_in_b": nrm(ks[2], (D_MODEL,), 0.02),
        "w_in": nrm(ks[3], (L, D_MODEL, IN_WIDTH), D_MODEL ** -0.5),
        "att_q_gain": 1.0 + nrm(ks[4], (L, HEAD_DIM), 0.02),
        "att_k_gain": 1.0 + nrm(ks[5], (L, HEAD_DIM), 0.02),
        "gla_gate_w": nrm(ks[6], (L, 2, GLA_GATE_RANK, GLA_QK_WIDTH), GLA_GATE_RANK ** -0.5),
        "gla_gate_b": nrm(ks[7], (L, 2, GLA_QK_WIDTH), 0.1),
        "gla_norm_g": 1.0 + nrm(ks[8], (L, GLA_DV), 0.02),
        "w_out": nrm(ks[9], (L, MIX_WIDTH, D_MODEL), MIX_WIDTH ** -0.5 * DN_BETA),
        "ln1_g": 1.0 + nrm(ks[10], (L, D_MODEL), 0.02),
        "ln1_b": nrm(ks[11], (L, D_MODEL), 0.02),
        "router_w": nrm(ks[12], (L, D_MODEL, N_EXPERTS), D_MODEL ** -0.5),
        "router_b": nrm(ks[13], (L, N_EXPERTS), 0.01),
        "exp_w_gu": nrm(ks[14], (L, N_EXPERTS, D_MODEL, 2 * D_EXPERT), D_MODEL ** -0.5),
        "exp_b_gu": nrm(ks[15], (L, N_EXPERTS, 2 * D_EXPERT), 0.02),
        "exp_w_down": nrm(ks[16], (L, N_EXPERTS, D_EXPERT, D_MODEL), D_EXPERT ** -0.5 * DN_BETA),
        "exp_b_down": nrm(ks[17], (L, N_EXPERTS, D_MODEL), 0.02),
        "ln2_g": 1.0 + nrm(ks[18], (L, D_MODEL), 0.02),
        "ln2_b": nrm(ks[19], (L, D_MODEL), 0.02),
    }


def reference(x, ln_in_g, ln_in_b, w_in, att_q_gain, att_k_gain, gla_gate_w, gla_gate_b,
              gla_norm_g, w_out, ln1_g, ln1_b, router_w, router_b, exp_w_gu, exp_b_gu,
              exp_w_down, exp_b_down, ln2_g, ln2_b):
    b, s, d = x.shape
    cos, sin = axial_rope_tables(s)
    h = layer_norm(x, ln_in_g, ln_in_b)
    for l in range(DEPTH):
        proj = jnp.einsum('bsd,dp->bsp', h, w_in[l])
        aq, ak, av, gq, gk, gv, gr, glr, fx = split_columns(proj)
        o_att = gqa_axial_attention(aq, ak, av, att_q_gain[l], att_k_gain[l], cos, sin)
        o_gla = gla_mixer(gq, gk, gv, gr, glr, gla_gate_w[l], gla_gate_b[l], gla_norm_g[l])
        o_fft = fourier_mixer(fx)
        mix = jnp.concatenate([o_att, o_gla, o_fft], axis=-1).astype(h.dtype)
        h = layer_norm(DN_ALPHA * h + mix @ w_out[l], ln1_g[l], ln1_b[l])
        y = moe_ffn(h, router_w[l], router_b[l], exp_w_gu[l], exp_b_gu[l], exp_w_down[l], exp_b_down[l])
        h = layer_norm(DN_ALPHA * h + y, ln2_g[l], ln2_b[l])
    return h
```

```python
import functools
import math

import numpy as np
import jax
import jax.numpy as jnp
from jax import lax
from jax.experimental import pallas as pl
from jax.experimental.pallas import tpu as pltpu

F32 = jnp.float32
BF16 = jnp.bfloat16

GRID_W = 64
HEAD_DIM = 64
ATT_HEADS = 8
ATT_KV_HEADS = 2
ATT_WIDTH = ATT_HEADS * HEAD_DIM
ATT_KV_WIDTH = ATT_KV_HEADS * HEAD_DIM
ROPE_THETA = 10000.0
ROPE_FREQS = HEAD_DIM // 4
GLA_HEADS = 4
GLA_DK = 32
GLA_DV = 64
GLA_QK_WIDTH = GLA_HEADS * GLA_DK
GLA_WIDTH = GLA_HEADS * GLA_DV
GLA_GATE_RANK = 16
GLA_TAU = 16.0
FNET_GROUPS = 4
FNET_GROUP_DIM = 64
FNET_WIDTH = FNET_GROUPS * FNET_GROUP_DIM
TOP_K = 4
SWIGLU_LIMIT = 7.0
SWIGLU_ALPHA = 1.702
LN_EPS = 1e-5
RMS_EPS = 1e-6

LANES = 128
VMEM_LIMIT_BYTES = 56 * 1024 * 1024

NT_DIMS = (((1,), (1,)), ((), ()))
TN_DIMS = (((0,), (0,)), ((), ()))


def _cparams(*sem):
    return pltpu.CompilerParams(dimension_semantics=sem, vmem_limit_bytes=VMEM_LIMIT_BYTES)


def _layer_norm_rows(x, g, b):
    mu = jnp.mean(x, axis=-1, keepdims=True)
    xc = x - mu
    var = jnp.mean(xc * xc, axis=-1, keepdims=True)
    return xc * lax.rsqrt(var + LN_EPS) * g + b


def _ln_kernel(x_ref, g_ref, b_ref, o_ref):
    o_ref[...] = _layer_norm_rows(x_ref[...], g_ref[...], b_ref[...])


def _layer_norm(x, g, b, tm):
    n, d = x.shape
    return pl.pallas_call(
        _ln_kernel,
        grid=(n // tm,),
        in_specs=[pl.BlockSpec((tm, d), lambda i: (i, 0)),
                  pl.BlockSpec((1, d), lambda i: (0, 0)),
                  pl.BlockSpec((1, d), lambda i: (0, 0))],
        out_specs=pl.BlockSpec((tm, d), lambda i: (i, 0)),
        out_shape=jax.ShapeDtypeStruct((n, d), F32),
        compiler_params=_cparams("parallel"),
        name="ln_in",
    )(x, g.reshape(1, d), b.reshape(1, d))


P_Q = 0
P_K = P_Q + ATT_WIDTH
P_V = P_K + 2 * ATT_KV_WIDTH
P_GQ = P_V + 4 * ATT_KV_WIDTH
P_GK = P_GQ + GLA_QK_WIDTH
P_GV = P_GK + GLA_QK_WIDTH
P_GR = P_GV + GLA_WIDTH
P_LR = P_GR + GLA_WIDTH
P_FX = P_LR + LANES
P_WIDTH = P_FX + FNET_WIDTH


def _pad_in_proj(w_in):
    l, d, _ = w_in.shape
    o = 0
    aq = w_in[..., o:o + ATT_WIDTH]; o += ATT_WIDTH
    ak = w_in[..., o:o + ATT_KV_WIDTH]; o += ATT_KV_WIDTH
    av = w_in[..., o:o + ATT_KV_WIDTH]; o += ATT_KV_WIDTH
    gq = w_in[..., o:o + GLA_QK_WIDTH]; o += GLA_QK_WIDTH
    gk = w_in[..., o:o + GLA_QK_WIDTH]; o += GLA_QK_WIDTH
    gv = w_in[..., o:o + GLA_WIDTH]; o += GLA_WIDTH
    gr = w_in[..., o:o + GLA_WIDTH]; o += GLA_WIDTH
    lr = w_in[..., o:o + 2 * GLA_GATE_RANK]; o += 2 * GLA_GATE_RANK
    fx = w_in[..., o:o + FNET_WIDTH]
    k0, k1 = ak[..., :HEAD_DIM], ak[..., HEAD_DIM:]
    v0, v1 = av[..., :HEAD_DIM], av[..., HEAD_DIM:]
    z64 = jnp.zeros((l, d, HEAD_DIM), w_in.dtype)
    zlr = jnp.zeros((l, d, LANES - 2 * GLA_GATE_RANK), w_in.dtype)
    cols = [aq, k0, k0, k1, k1, v0, z64, z64, v0, v1, z64, z64, v1, gq, gk, gv, gr, lr, zlr, fx]
    return jnp.concatenate(cols, axis=-1).astype(BF16)


def _proj_kernel(h_ref, w_ref, cos_ref, sin_ref, qg_ref, kg_ref, vpat_ref, bd_ref, gm_ref, gb_ref,
                 cc_ref, sc_ref,
                 q_ref, k2_ref, v4_ref, gq_ref, gk_ref, gv_ref, gr_ref, la_ref, zc_ref, zs_ref):
    acc = jnp.dot(h_ref[...].astype(BF16), w_ref[...], preferred_element_type=F32)
    cos = cos_ref[...]
    sin = sin_ref[...]
    bd = bd_ref[...]
    lane = lax.broadcasted_iota(jnp.int32, (1, LANES), 1)
    first_half = (lane % (2 * ROPE_FREQS)) < ROPE_FREQS

    def norm_rope(x, gain, scale):
        ss = jnp.dot((x * x).astype(BF16), bd, preferred_element_type=F32)
        xn = x * lax.rsqrt(ss * (1.0 / HEAD_DIM) + RMS_EPS) * gain
        partner = jnp.where(first_half, pltpu.roll(xn, LANES - ROPE_FREQS, 1), pltpu.roll(xn, ROPE_FREQS, 1))
        return (xn * cos + partner * sin) * scale

    for c in range(ATT_WIDTH // LANES):
        x = acc[:, P_Q + c * LANES:P_Q + (c + 1) * LANES]
        q_ref[:, c * LANES:(c + 1) * LANES] = norm_rope(x, qg_ref[...], HEAD_DIM ** -0.5).astype(BF16)
    for c in range(2 * ATT_KV_WIDTH // LANES):
        x = acc[:, P_K + c * LANES:P_K + (c + 1) * LANES]
        k2_ref[:, c * LANES:(c + 1) * LANES] = norm_rope(x, kg_ref[...], 1.0).astype(BF16)
    v4_ref[...] = (acc[:, P_V:P_V + 4 * ATT_KV_WIDTH] + vpat_ref[...]).astype(BF16)

    gq_ref[...] = (acc[:, P_GQ:P_GQ + GLA_QK_WIDTH] * (GLA_DK ** -0.5)).astype(BF16)
    gk_ref[...] = acc[:, P_GK:P_GK + GLA_QK_WIDTH].astype(BF16)
    gv_ref[...] = acc[:, P_GV:P_GV + GLA_WIDTH].astype(BF16)
    gr_ref[...] = acc[:, P_GR:P_GR + GLA_WIDTH].astype(BF16)
    z = jnp.dot(acc[:, P_LR:P_LR + LANES].astype(BF16), gm_ref[...], preferred_element_type=F32) + gb_ref[...]
    la_ref[...] = (jnp.minimum(z, 0.0) - jnp.log(1.0 + jnp.exp(-jnp.abs(z)))) * (1.0 / GLA_TAU)

    fx = acc[:, P_FX:P_FX + FNET_WIDTH].astype(BF16)
    zc_ref[...] = jnp.dot(fx, cc_ref[...], preferred_element_type=F32).astype(BF16)
    zs_ref[...] = jnp.dot(fx, sc_ref[...], preferred_element_type=F32).astype(BF16)


def _proj(h, w, cos, sin, qg, kg, vpat, bd, gm, gb, cc, sc, batch, seq, tm):
    n, d = h.shape
    nst = seq // tm
    row = lambda w_: pl.BlockSpec((tm, w_), lambda i: (i, 0))
    const = lambda a: pl.BlockSpec(a.shape, lambda i: (0,) * a.ndim)
    tab = pl.BlockSpec((tm, LANES), lambda i: (i % nst, 0))
    zspec = pl.BlockSpec((tm, FNET_WIDTH), lambda i: (i % nst, i // nst))
    outs = [(ATT_WIDTH, BF16), (2 * ATT_KV_WIDTH, BF16), (4 * ATT_KV_WIDTH, BF16), (GLA_QK_WIDTH, BF16),
            (GLA_QK_WIDTH, BF16), (GLA_WIDTH, BF16), (GLA_WIDTH, BF16), (2 * GLA_QK_WIDTH, F32)]
    return pl.pallas_call(
        _proj_kernel,
        grid=(n // tm,),
        in_specs=[row(d), const(w), tab, tab, const(qg), const(kg), const(vpat), const(bd), const(gm),
                  const(gb), const(cc), const(sc)],
        out_specs=[row(w_) for w_, _ in outs] + [zspec, zspec],
        out_shape=[jax.ShapeDtypeStruct((n, w_), dt) for w_, dt in outs]
        + [jax.ShapeDtypeStruct((seq, batch * FNET_WIDTH), BF16)] * 2,
        compiler_params=_cparams("parallel"),
        name="in_proj",
    )(h, w, cos, sin, qg, kg, vpat, bd, gm, gb, cc, sc)


def _attn_kernel(q_ref, k_ref, v_ref, o_ref):
    k2 = k_ref[0]
    lane = lax.broadcasted_iota(jnp.int32, (1, LANES), 1)
    low = lane < HEAD_DIM
    for p in range(2):
        qp = q_ref[0, :, p * LANES:(p + 1) * LANES]
        outs = []
        for hh in range(2):
            sel = low if hh == 0 else jnp.logical_not(low)
            qm = jnp.where(sel, qp, jnp.zeros_like(qp))
            s = lax.dot_general(qm, k2, NT_DIMS, preferred_element_type=F32)
            m = jnp.max(s, axis=-1, keepdims=True)
            pe = jnp.exp(s - m).astype(BF16)
            o = jnp.dot(pe, v_ref[0, :, hh * LANES:(hh + 1) * LANES], preferred_element_type=F32)
            den = o[:, HEAD_DIM:HEAD_DIM + 1] if hh == 0 else o[:, 0:1]
            outs.append(o / den)
        o_ref[0, :, p * LANES:(p + 1) * LANES] = jnp.where(low, outs[0], outs[1]).astype(BF16)


def _attention(q, k2, v4, tq):
    b, s, _ = q.shape
    gw = ATT_WIDTH // ATT_KV_HEADS
    return pl.pallas_call(
        _attn_kernel,
        grid=(b, ATT_KV_HEADS, s // tq),
        in_specs=[pl.BlockSpec((1, tq, gw), lambda bi, g, qi: (bi, qi, g)),
                  pl.BlockSpec((1, s, LANES), lambda bi, g, qi: (bi, 0, g)),
                  pl.BlockSpec((1, s, 2 * LANES), lambda bi, g, qi: (bi, 0, g))],
        out_specs=pl.BlockSpec((1, tq, gw), lambda bi, g, qi: (bi, qi, g)),
        out_shape=jax.ShapeDtypeStruct((b, s, ATT_WIDTH), BF16),
        compiler_params=_cparams("parallel", "parallel", "parallel"),
        name="gqa_attention",
    )(q, k2, v4)


GLA_TILE = 128


def _gla_constants(t):
    nl = int(math.log2(t))
    i = np.arange(t)[:, None]
    m = np.arange(t)[None, :]
    fwd = [m <= i, m > i]
    bwd = [m >= i, m < i]
    mask_f, mask_b = [], []
    for l in range(nl):
        s = t >> l
        half = s // 2
        blk = (i // s) * s
        ref = blk + half - 1
        right = (i - blk) >= half
        wf = np.where(right, (m > ref) & (m <= i), (m > i) & (m <= ref))
        wb = np.where(right, (m > ref) & (m < i), (m >= i) & (m <= ref))
        fwd.append(wf)
        bwd.append(wb)
        same = (i // s) == (m // s)
        mask_f.append(same)
        mask_b.append(same)
    mask_f.append(i == m)
    wst = np.stack([np.concatenate(fwd, 0), np.concatenate(bwd, 0)]).astype(np.float32)
    tile4 = lambda a: np.tile(a.astype(np.float32), (GLA_HEADS, 1))
    mf = np.stack([tile4(a) for a in mask_f])
    mb = np.stack([tile4(a) for a in mask_b])
    return nl, wst, mf, mb


def _gla_kernel(nl, t, q_ref, k_ref, v_ref, la_ref, r_ref, w_ref, mf_ref, mb_ref, bdm_ref, bdn_ref, g_ref,
                o_ref, st_ref, acc_ref):
    s = q_ref.shape[1]
    nt = s // t
    row = lax.broadcasted_iota(jnp.int32, (t, GLA_QK_WIDTH), 0)
    lane_k = lax.broadcasted_iota(jnp.int32, (1, GLA_QK_WIDTH), 1)
    lane_v = lax.broadcasted_iota(jnp.int32, (1, GLA_WIDTH), 1)
    head_k = [(lane_k // GLA_DK) == h for h in range(GLA_HEADS)]
    head_v = [(lane_v // GLA_DV) == h for h in range(GLA_HEADS)]

    def tile_out(t0, d):
        q = q_ref[0, pl.ds(t0, t), :].astype(F32)
        k = k_ref[0, pl.ds(t0, t), :].astype(F32)
        v = v_ref[0, pl.ds(t0, t), :]
        la = la_ref[0, pl.ds(t0, t), d * GLA_QK_WIDTH:(d + 1) * GLA_QK_WIDTH]
        la_hi = la.astype(BF16)
        la_lo = (la - la_hi.astype(F32)).astype(BF16)
        wst = w_ref[d]
        x = jnp.exp(jnp.dot(wst, la_hi, preferred_element_type=F32)
                    + jnp.dot(wst, la_lo, preferred_element_type=F32))
        xq = x[0:t]
        q_st = (q * xq).astype(BF16)
        k_st = (k * x[t:2 * t]).astype(BF16)
        dec = xq[t - 1:t] if d == 0 else xq[0:1]
        st = st_ref[...]
        o_inter = lax.dot_general(q_st, st.astype(BF16), NT_DIMS, preferred_element_type=F32)
        kv = lax.dot_general(v, k_st, TN_DIMS, preferred_element_type=F32)
        st_ref[...] = st * dec + kv * bdm_ref[...]

        sc = jnp.zeros((GLA_HEADS * t, t), F32)
        m_ref = mf_ref if d == 0 else mb_ref
        for l in range(nl):
            half = t >> (l + 1)
            right = ((row // half) % 2) == 1
            qside = right if d == 0 else jnp.logical_not(right)
            g = jnp.where(qside, q, k) * x[(2 + l) * t:(3 + l) * t]
            a = jnp.where(qside, g, 0.0)
            b = jnp.where(qside, 0.0, g).astype(BF16)
            a4 = jnp.concatenate([jnp.where(head_k[h], a, 0.0) for h in range(GLA_HEADS)], axis=0).astype(BF16)
            sc = sc + lax.dot_general(a4, b, NT_DIMS, preferred_element_type=F32) * m_ref[l]
        if d == 0:
            a4 = jnp.concatenate([jnp.where(head_k[h], q, 0.0) for h in range(GLA_HEADS)], axis=0).astype(BF16)
            sc = sc + lax.dot_general(a4, k.astype(BF16), NT_DIMS, preferred_element_type=F32) * m_ref[nl]
        o4 = jnp.dot(sc.astype(BF16), v, preferred_element_type=F32)
        o = o_inter
        for h in range(GLA_HEADS):
            o = o + jnp.where(head_v[h], o4[h * t:(h + 1) * t], 0.0)
        return o

    st_ref[...] = jnp.zeros_like(st_ref)

    def fwd_body(i, c):
        t0 = pl.multiple_of(i * t, t)
        acc_ref[pl.ds(t0, t), :] = tile_out(t0, 0)
        return c

    lax.fori_loop(0, nt, fwd_body, 0)
    st_ref[...] = jnp.zeros_like(st_ref)

    def bwd_body(i, c):
        t0 = pl.multiple_of((nt - 1 - i) * t, t)
        tot = acc_ref[pl.ds(t0, t), :] + tile_out(t0, 1)
        ss = jnp.dot((tot * tot).astype(BF16), bdn_ref[...], preferred_element_type=F32)
        r = r_ref[0, pl.ds(t0, t), :].astype(F32)
        y = tot * lax.rsqrt(ss * (1.0 / GLA_DV) + RMS_EPS) * g_ref[...] * (r / (1.0 + jnp.exp(-r)))
        o_ref[0, pl.ds(t0, t), :] = y.astype(BF16)
        return c

    lax.fori_loop(0, nt, bwd_body, 0)


def _gla(gq, gk, gv, la, gr, wst, mf, mb, bdm, bdn, g, nl, t):
    b, s, _ = gq.shape
    seq = lambda w_: pl.BlockSpec((1, s, w_), lambda bi: (bi, 0, 0))
    const = lambda a: pl.BlockSpec(a.shape, lambda bi: (0,) * a.ndim)
    return pl.pallas_call(
        functools.partial(_gla_kernel, nl, t),
        grid=(b,),
        in_specs=[seq(GLA_QK_WIDTH), seq(GLA_QK_WIDTH), seq(GLA_WIDTH), seq(2 * GLA_QK_WIDTH), seq(GLA_WIDTH),
                  const(wst), const(mf), const(mb), const(bdm), const(bdn), const(g)],
        out_specs=seq(GLA_WIDTH),
        out_shape=jax.ShapeDtypeStruct((b, s, GLA_WIDTH), BF16),
        scratch_shapes=[pltpu.VMEM((GLA_WIDTH, GLA_QK_WIDTH), F32), pltpu.VMEM((s, GLA_WIDTH), F32)],
        compiler_params=_cparams("parallel"),
        name="gla_scan",
    )(gq, gk, gv, la, gr, wst, mf, mb, bdm, bdn, g)


def _dft_kernel(fc_ref, fs_ref, zc_ref, zs_ref, o_ref):
    o_ref[...] = (jnp.dot(fc_ref[...], zc_ref[...], preferred_element_type=F32)
                  + jnp.dot(fs_ref[...], zs_ref[...], preferred_element_type=F32)).astype(BF16)


def _position_dft(fc, fs, zc, zs, tm, tn):
    s, w = zc.shape
    return pl.pallas_call(
        _dft_kernel,
        grid=(w // tn, s // tm),
        in_specs=[pl.BlockSpec((tm, s), lambda j, i: (i, 0)),
                  pl.BlockSpec((tm, s), lambda j, i: (i, 0)),
                  pl.BlockSpec((s, tn), lambda j, i: (0, j)),
                  pl.BlockSpec((s, tn), lambda j, i: (0, j))],
        out_specs=pl.BlockSpec((tm, tn), lambda j, i: (i, j)),
        out_shape=jax.ShapeDtypeStruct((s, w), BF16),
        compiler_params=_cparams("parallel", "parallel"),
        name="position_dft",
    )(fc, fs, zc, zs)


def _outproj_kernel(alpha, n_exp, oatt_ref, ogla_ref, offt_ref, h_ref, wo_ref, g_ref, b_ref,
                    wrh_ref, wrl_ref, rb_ref, us_ref,
                    h1_ref, idx_ref, wts_ref, rank_ref, cnt_ref, carry_ref):
    i = pl.program_id(0)

    @pl.when(i == 0)
    def _():
        carry_ref[...] = jnp.zeros_like(carry_ref)

    acc = jnp.dot(oatt_ref[...], wo_ref[0:ATT_WIDTH, :], preferred_element_type=F32)
    acc = acc + jnp.dot(ogla_ref[...], wo_ref[ATT_WIDTH:ATT_WIDTH + GLA_WIDTH, :], preferred_element_type=F32)
    acc = acc + jnp.dot(offt_ref[...], wo_ref[ATT_WIDTH + GLA_WIDTH:, :], preferred_element_type=F32)
    h1 = _layer_norm_rows(alpha * h_ref[...] + acc, g_ref[...], b_ref[...])
    h1_ref[...] = h1

    hh = h1.astype(BF16)
    hl = (h1 - hh.astype(F32)).astype(BF16)
    wrh = wrh_ref[...]
    logits = (lax.dot_general(wrh, hh, NT_DIMS, preferred_element_type=F32)
              + lax.dot_general(wrh, hl, NT_DIMS, preferred_element_type=F32)
              + lax.dot_general(wrl_ref[...], hh, NT_DIMS, preferred_element_type=F32)) + rb_ref[...]
    tm = logits.shape[1]
    e_iota = lax.broadcasted_iota(jnp.int32, (n_exp, tm), 0)
    cur = logits
    vals, idxs, sels = [], [], []
    for _k in range(TOP_K):
        m = jnp.max(cur, axis=0, keepdims=True)
        ik = jnp.min(jnp.where(cur == m, e_iota, n_exp), axis=0, keepdims=True)
        sel = e_iota == ik
        vals.append(m)
        idxs.append(ik)
        sels.append(sel)
        cur = jnp.where(sel, -jnp.inf, cur)
    ex = [jnp.exp(v - vals[0]) for v in vals]
    den = ex[0] + ex[1] + ex[2] + ex[3]
    zero = jnp.zeros_like(den)
    idx_ref[...] = jnp.concatenate(idxs, axis=0)
    wts_ref[...] = jnp.concatenate([e / den for e in ex] + [zero] * (wts_ref.shape[0] - TOP_K), axis=0)

    onehot = jnp.zeros((n_exp, tm), F32)
    for sel in sels:
        onehot = onehot + sel.astype(F32)
    before = jnp.dot(onehot.astype(BF16), us_ref[...], preferred_element_type=F32) + carry_ref[:, 0:1]
    ranks = [jnp.sum(jnp.where(sel, before, 0.0), axis=0, keepdims=True) for sel in sels]
    rank_ref[...] = jnp.concatenate(ranks, axis=0).astype(jnp.int32)
    carry = carry_ref[...] + jnp.sum(onehot, axis=1, keepdims=True)
    carry_ref[...] = carry
    cnt_ref[...] = carry.astype(jnp.int32)


def _outproj(alpha, o_att, o_gla, o_fft, h, wo, g, b, wrh, wrl, rb, us, seq, tm):
    n, d = h.shape
    n_exp = wrh.shape[0]
    nst = seq // tm
    row = lambda w_: pl.BlockSpec((tm, w_), lambda i: (i, 0))
    const = lambda a: pl.BlockSpec(a.shape, lambda i: (0,) * a.ndim)
    tok = lambda r: pl.BlockSpec((r, tm), lambda i: (0, i))
    return pl.pallas_call(
        functools.partial(_outproj_kernel, alpha, n_exp),
        grid=(n // tm,),
        in_specs=[row(ATT_WIDTH), row(GLA_WIDTH),
                  pl.BlockSpec((tm, FNET_WIDTH), lambda i: (i % nst, i // nst)),
                  row(d), const(wo), const(g), const(b), const(wrh), const(wrl), const(rb), const(us)],
        out_specs=[row(d), tok(TOP_K), tok(2 * TOP_K), tok(TOP_K),
                   pl.BlockSpec((n_exp, LANES), lambda i: (0, 0))],
        out_shape=[jax.ShapeDtypeStruct((n, d), F32),
                   jax.ShapeDtypeStruct((TOP_K, n), jnp.int32),
                   jax.ShapeDtypeStruct((2 * TOP_K, n), F32),
                   jax.ShapeDtypeStruct((TOP_K, n), jnp.int32),
                   jax.ShapeDtypeStruct((n_exp, LANES), jnp.int32)],
        scratch_shapes=[pltpu.VMEM((n_exp, LANES), F32)],
        compiler_params=_cparams("arbitrary"),
        name="out_proj_router",
    )(o_att, o_gla, o_fft, h, wo, g, b, wrh, wrl, rb, us)


DMA_UNROLL = 8


def _row_copy(src, si, dst, di, sem):
    return pltpu.make_async_copy(src.at[pl.ds(si, 1)], dst.at[pl.ds(di, 1)], sem)


def _dispatch_kernel(tb, dest_ref, h_ref, xs_in_ref, xs_ref, sem):
    del xs_in_ref
    base = pl.program_id(0) * tb

    def start(t, c):
        for k in range(TOP_K):
            _row_copy(h_ref, base + t, xs_ref, dest_ref[k, t], sem).start()
        return c

    lax.fori_loop(0, tb, start, 0, unroll=DMA_UNROLL)

    def wait(t, c):
        for k in range(TOP_K):
            _row_copy(h_ref, 0, xs_ref, 0, sem).wait()
        return c

    lax.fori_loop(0, tb, wait, 0, unroll=DMA_UNROLL)


def _dispatch(dest, h, rows, tb):
    n, d = h.shape
    zeros = jnp.zeros((rows, d), F32)
    return pl.pallas_call(
        functools.partial(_dispatch_kernel, tb),
        grid=(n // tb,),
        in_specs=[pl.BlockSpec((TOP_K, tb), lambda i: (0, i), memory_space=pltpu.SMEM),
                  pl.BlockSpec(memory_space=pl.ANY),
                  pl.BlockSpec(memory_space=pl.ANY)],
        out_specs=pl.BlockSpec(memory_space=pl.ANY),
        out_shape=jax.ShapeDtypeStruct((rows, d), F32),
        scratch_shapes=[pltpu.SemaphoreType.DMA],
        input_output_aliases={2: 0},
        compiler_params=_cparams("arbitrary"),
        name="moe_dispatch",
    )(dest, h, zeros)


def _ffn_kernel(te_ref, tv_ref, x_ref, wgu_ref, bgu_ref, wd_ref, bd_ref, y_ref, wgu_bf, wd_bf):
    j = pl.program_id(0)
    valid = tv_ref[j]
    new_expert = jnp.logical_or(j == 0, te_ref[j] != te_ref[jnp.maximum(j - 1, 0)])

    @pl.when(jnp.logical_and(new_expert, valid > 0))
    def _():
        wgu_bf[...] = wgu_ref[0].astype(BF16)
        wd_bf[...] = wd_ref[0].astype(BF16)

    @pl.when(valid > 0)
    def _():
        de = wd_bf.shape[0]
        gu = jnp.dot(x_ref[...].astype(BF16), wgu_bf[...], preferred_element_type=F32) + bgu_ref[0]
        gate = jnp.minimum(gu[:, :de], SWIGLU_LIMIT)
        up = jnp.clip(gu[:, de:], -SWIGLU_LIMIT, SWIGLU_LIMIT)
        hid = (up + 1.0) * (gate / (1.0 + jnp.exp(-SWIGLU_ALPHA * gate)))
        y_ref[...] = jnp.dot(hid.astype(BF16), wd_bf[...], preferred_element_type=F32) + bd_ref[0]

    @pl.when(valid == 0)
    def _():
        y_ref[...] = jnp.zeros_like(y_ref)


def _expert_ffn(tile_expert, tile_valid, xs, wgu, bgu, wd, bd, tmf):
    rows, d = xs.shape
    n_exp, _, de2 = wgu.shape
    de = de2 // 2
    grid_spec = pltpu.PrefetchScalarGridSpec(
        num_scalar_prefetch=2,
        grid=(rows // tmf,),
        in_specs=[pl.BlockSpec((tmf, d), lambda j, te, tv: (j, 0)),
                  pl.BlockSpec((1, d, de2), lambda j, te, tv: (te[j], 0, 0)),
                  pl.BlockSpec((1, 1, de2), lambda j, te, tv: (te[j], 0, 0)),
                  pl.BlockSpec((1, de, d), lambda j, te, tv: (te[j], 0, 0)),
                  pl.BlockSpec((1, 1, d), lambda j, te, tv: (te[j], 0, 0))],
        out_specs=pl.BlockSpec((tmf, d), lambda j, te, tv: (j, 0)),
        scratch_shapes=[pltpu.VMEM((d, de2), BF16), pltpu.VMEM((de, d), BF16)],
    )
    return pl.pallas_call(
        _ffn_kernel,
        grid_spec=grid_spec,
        out_shape=jax.ShapeDtypeStruct((rows, d), F32),
        compiler_params=_cparams("arbitrary"),
        name="moe_expert_ffn",
    )(tile_expert, tile_valid, xs, wgu, bgu.reshape(n_exp, 1, de2), wd, bd.reshape(n_exp, 1, d))


def _combine_kernel(alpha, tb, dest_ref, w_ref, h1_ref, g_ref, b_ref, y_ref, o_ref, ybuf, sem):
    def start(t, c):
        for k in range(TOP_K):
            _row_copy(y_ref, dest_ref[k, t], ybuf.at[k], t, sem).start()
        return c

    lax.fori_loop(0, tb, start, 0, unroll=DMA_UNROLL)

    def wait(t, c):
        for k in range(TOP_K):
            _row_copy(y_ref, 0, ybuf.at[k], 0, sem).wait()
        return c

    lax.fori_loop(0, tb, wait, 0, unroll=DMA_UNROLL)

    wt = jnp.transpose(w_ref[...])
    acc = alpha * h1_ref[...]
    for k in range(TOP_K):
        acc = acc + ybuf[k] * wt[:, k:k + 1]
    o_ref[...] = _layer_norm_rows(acc, g_ref[...], b_ref[...])


def _combine(alpha, dest, wts, h1, g, b, ys, tb):
    n, d = h1.shape
    return pl.pallas_call(
        functools.partial(_combine_kernel, alpha, tb),
        grid=(n // tb,),
        in_specs=[pl.BlockSpec((TOP_K, tb), lambda i: (0, i), memory_space=pltpu.SMEM),
                  pl.BlockSpec((2 * TOP_K, tb), lambda i: (0, i)),
                  pl.BlockSpec((tb, d), lambda i: (i, 0)),
                  pl.BlockSpec((1, d), lambda i: (0, 0)),
                  pl.BlockSpec((1, d), lambda i: (0, 0)),
                  pl.BlockSpec(memory_space=pl.ANY)],
        out_specs=pl.BlockSpec((tb, d), lambda i: (i, 0)),
        out_shape=jax.ShapeDtypeStruct((n, d), F32),
        scratch_shapes=[pltpu.VMEM((TOP_K, tb, d), F32), pltpu.SemaphoreType.DMA],
        compiler_params=_cparams("arbitrary"),
        name="moe_combine",
    )(dest, wts, h1, g.reshape(1, d), b.reshape(1, d), ys)


def _rope_tables(seq):
    pos = jnp.arange(seq)
    row_id = (pos // GRID_W).astype(F32)
    col_id = (pos % GRID_W).astype(F32)
    inv_freq = 1.0 / (ROPE_THETA ** (jnp.arange(ROPE_FREQS, dtype=F32) / ROPE_FREQS))
    lane = np.arange(LANES)
    hd = lane % HEAD_DIM
    freq = hd % ROPE_FREQS
    use_col = (hd // (2 * ROPE_FREQS)) == 1
    second = (hd % (2 * ROPE_FREQS)) >= ROPE_FREQS
    ang = jnp.where(use_col[None, :], col_id[:, None], row_id[:, None]) * inv_freq[freq][None, :]
    sign = np.where(second, 1.0, -1.0).astype(np.float32)
    return jnp.cos(ang), jnp.sin(ang) * sign[None, :]


def _dft_tables(seq):
    idx = jnp.arange(seq, dtype=jnp.int32)
    prod = (idx[:, None] * idx[None, :]) % seq
    ang = prod.astype(F32) * (2.0 * math.pi / seq)
    scale = seq ** -0.5
    fc = (jnp.cos(ang) * scale).astype(BF16)
    fs = (jnp.sin(ang) * -scale).astype(BF16)
    c = np.arange(FNET_GROUP_DIM)
    angc = 2.0 * np.pi * ((c[:, None] * c[None, :]) % FNET_GROUP_DIM) / FNET_GROUP_DIM
    eye = np.eye(FNET_GROUPS)
    cc = np.kron(eye, np.cos(angc)) * FNET_GROUP_DIM ** -0.5
    sc = np.kron(eye, np.sin(angc)) * FNET_GROUP_DIM ** -0.5
    return fc, fs, jnp.asarray(cc, BF16), jnp.asarray(sc, BF16)


def _block_ones(width, group):
    g = np.arange(width) // group
    return jnp.asarray((g[:, None] == g[None, :]).astype(np.float32), BF16)


def _pick(limit, n):
    t = min(limit, n)
    while n % t:
        t //= 2
    return t


def kernel(x, ln_in_g, ln_in_b, w_in, att_q_gain, att_k_gain, gla_gate_w, gla_gate_b, gla_norm_g, w_out,
           ln1_g, ln1_b, router_w, router_b, exp_w_gu, exp_b_gu, exp_w_down, exp_b_down, ln2_g, ln2_b):
    batch, seq, d = x.shape
    n = batch * seq
    depth = w_in.shape[0]
    n_exp = router_w.shape[-1]
    alpha = (2.0 * depth) ** 0.25

    tm = _pick(512, seq)
    tq = _pick(256, seq)
    tg = _pick(GLA_TILE, seq)
    tmf = _pick(512, n)
    tb = _pick(256, n)
    n_tiles = (TOP_K * n) // tmf + n_exp
    rows = n_tiles * tmf

    cos, sin = _rope_tables(seq)
    fc, fs, cc, sc = _dft_tables(seq)
    bd_head = _block_ones(LANES, HEAD_DIM)
    bd_gla = _block_ones(GLA_WIDTH, GLA_DV)
    vpat = np.zeros((1, 4 * ATT_KV_WIDTH), np.float32)
    vpat[0, [HEAD_DIM, LANES, 2 * LANES + HEAD_DIM, 3 * LANES]] = 1.0
    vpat = jnp.asarray(vpat)
    nl, wst, mf, mb = _gla_constants(tg)
    wst = jnp.asarray(wst, BF16)
    mf = jnp.asarray(mf)
    mb = jnp.asarray(mb)
    hv = np.arange(GLA_WIDTH) // GLA_DV
    hk = np.arange(GLA_QK_WIDTH) // GLA_DK
    bdm = jnp.asarray((hv[:, None] == hk[None, :]).astype(np.float32))
    us = np.arange(tm)
    us = jnp.asarray((us[:, None] < us[None, :]).astype(np.float32), BF16)

    w_pad = _pad_in_proj(w_in)
    qg = jnp.tile(att_q_gain, (1, LANES // HEAD_DIM)).reshape(depth, 1, LANES)
    kg = jnp.tile(att_k_gain, (1, LANES // HEAD_DIM)).reshape(depth, 1, LANES)
    gm = jnp.zeros((depth, LANES, 2 * GLA_QK_WIDTH), F32)
    gm = gm.at[:, :GLA_GATE_RANK, :GLA_QK_WIDTH].set(gla_gate_w[:, 0])
    gm = gm.at[:, GLA_GATE_RANK:2 * GLA_GATE_RANK, GLA_QK_WIDTH:].set(gla_gate_w[:, 1]).astype(BF16)
    gb = gla_gate_b.reshape(depth, 1, 2 * GLA_QK_WIDTH)
    gn = jnp.tile(gla_norm_g, (1, GLA_HEADS)).reshape(depth, 1, GLA_WIDTH)
    wo = w_out.astype(BF16)
    wr_t = jnp.swapaxes(router_w, 1, 2)
    wrh = wr_t.astype(BF16)
    wrl = (wr_t - wrh.astype(F32)).astype(BF16)
    rb = router_b.reshape(depth, n_exp, 1)

    h = _layer_norm(x.reshape(n, d), ln_in_g, ln_in_b, tm)
    for l in range(depth):
        q, k2, v4, gq, gk, gv, gr, la, zc, zs = _proj(
            h, w_pad[l], cos, sin, qg[l], kg[l], vpat, bd_head, gm[l], gb[l], cc, sc, batch, seq, tm)
        o_att = _attention(q.reshape(batch, seq, -1), k2.reshape(batch, seq, -1), v4.reshape(batch, seq, -1), tq)
        o_gla = _gla(gq.reshape(batch, seq, -1), gk.reshape(batch, seq, -1), gv.reshape(batch, seq, -1),
                     la.reshape(batch, seq, -1), gr.reshape(batch, seq, -1), wst, mf, mb, bdm, bd_gla, gn[l], nl, tg)
        o_fft = _position_dft(fc, fs, zc, zs, tm, _pick(512, batch * FNET_WIDTH))
        h1, idx, wts, rank, cnt = _outproj(
            alpha, o_att.reshape(n, -1), o_gla.reshape(n, -1), o_fft, h, wo[l], ln1_g[l].reshape(1, d),
            ln1_b[l].reshape(1, d), wrh[l], wrl[l], rb[l], us, seq, tm)

        counts = cnt[:, 0]
        tiles_per = (counts + tmf - 1) // tmf
        tile_end = jnp.cumsum(tiles_per)
        offs = (tile_end - tiles_per) * tmf
        dest = offs[idx] + rank
        tj = jnp.arange(n_tiles, dtype=jnp.int32)
        used = tile_end[-1]
        te_raw = jnp.searchsorted(tile_end, tj, side="right").astype(jnp.int32)
        last_e = jnp.searchsorted(tile_end, used - 1, side="right").astype(jnp.int32)
        tile_expert = jnp.where(tj < used, te_raw, last_e)
        tile_expert = jnp.minimum(tile_expert, n_exp - 1)
        tile_valid = jnp.where(tj < used, jnp.clip(counts[tile_expert] - (tj * tmf - offs[tile_expert]), 0, tmf), 0)
        tile_valid = tile_valid.astype(jnp.int32)

        xs = _dispatch(dest, h1, rows, tb)
        ys = _expert_ffn(tile_expert, tile_valid, xs, exp_w_gu[l], exp_b_gu[l], exp_w_down[l], exp_b_down[l], tmf)
        h = _combine(alpha, dest, wts, h1, ln2_g[l], ln2_b[l], ys, tb)
    return h.reshape(batch, seq, d)
```

```python
import functools
import math

import numpy as np
import jax
import jax.numpy as jnp
from jax import lax
from jax.experimental import pallas as pl
from jax.experimental.pallas import tpu as pltpu

F32 = jnp.float32
BF16 = jnp.bfloat16

GRID_W = 64
HEAD_DIM = 64
ATT_HEADS = 8
ATT_KV_HEADS = 2
ATT_WIDTH = ATT_HEADS * HEAD_DIM
ATT_KV_WIDTH = ATT_KV_HEADS * HEAD_DIM
ROPE_THETA = 10000.0
ROPE_FREQS = HEAD_DIM // 4
GLA_HEADS = 4
GLA_DK = 32
GLA_DV = 64
GLA_QK_WIDTH = GLA_HEADS * GLA_DK
GLA_WIDTH = GLA_HEADS * GLA_DV
GLA_GATE_RANK = 16
GLA_TAU = 16.0
FNET_GROUPS = 4
FNET_GROUP_DIM = 64
FNET_WIDTH = FNET_GROUPS * FNET_GROUP_DIM
TOP_K = 4
SWIGLU_LIMIT = 7.0
SWIGLU_ALPHA = 1.702
LN_EPS = 1e-5
RMS_EPS = 1e-6

LANES = 128
VMEM_LIMIT_BYTES = 56 * 1024 * 1024

NT_DIMS = (((1,), (1,)), ((), ()))
TN_DIMS = (((0,), (0,)), ((), ()))


def _cparams(*sem):
    return pltpu.CompilerParams(dimension_semantics=sem, vmem_limit_bytes=VMEM_LIMIT_BYTES)


def _layer_norm_rows(x, g, b):
    mu = jnp.mean(x, axis=-1, keepdims=True)
    xc = x - mu
    var = jnp.mean(xc * xc, axis=-1, keepdims=True)
    return xc * lax.rsqrt(var + LN_EPS) * g + b


def _ln_kernel(x_ref, g_ref, b_ref, o_ref):
    o_ref[...] = _layer_norm_rows(x_ref[...], g_ref[...], b_ref[...])


def _layer_norm(x, g, b, tm):
    n, d = x.shape
    return pl.pallas_call(
        _ln_kernel,
        grid=(n // tm,),
        in_specs=[pl.BlockSpec((tm, d), lambda i: (i, 0)),
                  pl.BlockSpec((1, d), lambda i: (0, 0)),
                  pl.BlockSpec((1, d), lambda i: (0, 0))],
        out_specs=pl.BlockSpec((tm, d), lambda i: (i, 0)),
        out_shape=jax.ShapeDtypeStruct((n, d), F32),
        compiler_params=_cparams("parallel"),
        name="ln_in",
    )(x, g.reshape(1, d), b.reshape(1, d))


P_Q = 0
P_K = P_Q + ATT_WIDTH
P_V = P_K + 2 * ATT_KV_WIDTH
P_GQ = P_V + 4 * ATT_KV_WIDTH
P_GK = P_GQ + GLA_QK_WIDTH
P_GV = P_GK + GLA_QK_WIDTH
P_GR = P_GV + GLA_WIDTH
P_LR = P_GR + GLA_WIDTH
P_FX = P_LR + LANES
P_WIDTH = P_FX + FNET_WIDTH

Q_SCALE = HEAD_DIM ** -0.5 * math.log2(math.e)


def _pad_in_proj(w_in):
    l, d, _ = w_in.shape
    o = 0
    aq = w_in[..., o:o + ATT_WIDTH]; o += ATT_WIDTH
    ak = w_in[..., o:o + ATT_KV_WIDTH]; o += ATT_KV_WIDTH
    av = w_in[..., o:o + ATT_KV_WIDTH]; o += ATT_KV_WIDTH
    gq = w_in[..., o:o + GLA_QK_WIDTH]; o += GLA_QK_WIDTH
    gk = w_in[..., o:o + GLA_QK_WIDTH]; o += GLA_QK_WIDTH
    gv = w_in[..., o:o + GLA_WIDTH]; o += GLA_WIDTH
    gr = w_in[..., o:o + GLA_WIDTH]; o += GLA_WIDTH
    lr = w_in[..., o:o + 2 * GLA_GATE_RANK]; o += 2 * GLA_GATE_RANK
    fx = w_in[..., o:o + FNET_WIDTH]
    k0, k1 = ak[..., :HEAD_DIM], ak[..., HEAD_DIM:]
    v0, v1 = av[..., :HEAD_DIM], av[..., HEAD_DIM:]
    z64 = jnp.zeros((l, d, HEAD_DIM), w_in.dtype)
    zlr = jnp.zeros((l, d, LANES - 2 * GLA_GATE_RANK), w_in.dtype)
    cols = [aq, k0, k0, k1, k1, v0, z64, z64, v0, v1, z64, z64, v1, gq, gk, gv, gr, lr, zlr, fx]
    return jnp.concatenate(cols, axis=-1).astype(BF16)


def _proj_kernel(h_ref, w_ref, cos_ref, sin_ref, qg_ref, kg_ref, vpat_ref, bd_ref, gm_ref, gb_ref,
                 cc_ref, sc_ref,
                 q_ref, k2_ref, v4_ref, gq_ref, gk_ref, gv_ref, gr_ref, la_ref, zc_ref, zs_ref):
    acc = jnp.dot(h_ref[...].astype(BF16), w_ref[...], preferred_element_type=F32)
    cos = cos_ref[...]
    sin = sin_ref[...]
    bd = bd_ref[...]
    lane = lax.broadcasted_iota(jnp.int32, (1, LANES), 1)
    first_half = (lane % (2 * ROPE_FREQS)) < ROPE_FREQS

    def norm_rope(x, gain, scale):
        ss = jnp.dot((x * x).astype(BF16), bd, preferred_element_type=F32)
        xn = x * lax.rsqrt(ss * (1.0 / HEAD_DIM) + RMS_EPS) * gain
        partner = jnp.where(first_half, pltpu.roll(xn, LANES - ROPE_FREQS, 1), pltpu.roll(xn, ROPE_FREQS, 1))
        return (xn * cos + partner * sin) * scale

    for c in range(ATT_WIDTH // LANES):
        x = acc[:, P_Q + c * LANES:P_Q + (c + 1) * LANES]
        q_ref[:, c * LANES:(c + 1) * LANES] = norm_rope(x, qg_ref[...], Q_SCALE).astype(BF16)
    for c in range(2 * ATT_KV_WIDTH // LANES):
        x = acc[:, P_K + c * LANES:P_K + (c + 1) * LANES]
        k2_ref[:, c * LANES:(c + 1) * LANES] = norm_rope(x, kg_ref[...], 1.0).astype(BF16)
    v4_ref[...] = (acc[:, P_V:P_V + 4 * ATT_KV_WIDTH] + vpat_ref[...]).astype(BF16)

    gq_ref[...] = (acc[:, P_GQ:P_GQ + GLA_QK_WIDTH] * (GLA_DK ** -0.5)).astype(BF16)
    gk_ref[...] = acc[:, P_GK:P_GK + GLA_QK_WIDTH].astype(BF16)
    gv_ref[...] = acc[:, P_GV:P_GV + GLA_WIDTH].astype(BF16)
    gr_ref[...] = acc[:, P_GR:P_GR + GLA_WIDTH].astype(BF16)
    z = jnp.dot(acc[:, P_LR:P_LR + LANES].astype(BF16), gm_ref[...], preferred_element_type=F32) + gb_ref[...]
    la_ref[...] = (jnp.minimum(z, 0.0) - jnp.log(1.0 + jnp.exp(-jnp.abs(z)))) * (1.0 / GLA_TAU)

    fx = acc[:, P_FX:P_FX + FNET_WIDTH].astype(BF16)
    zc_ref[...] = jnp.dot(fx, cc_ref[...], preferred_element_type=F32).astype(BF16)
    zs_ref[...] = jnp.dot(fx, sc_ref[...], preferred_element_type=F32).astype(BF16)


def _proj(h, w, cos, sin, qg, kg, vpat, bd, gm, gb, cc, sc, batch, seq, tm):
    n, d = h.shape
    nst = seq // tm
    row = lambda w_: pl.BlockSpec((tm, w_), lambda i: (i, 0))
    const = lambda a: pl.BlockSpec(a.shape, lambda i: (0,) * a.ndim)
    tab = pl.BlockSpec((tm, LANES), lambda i: (i % nst, 0))
    zspec = pl.BlockSpec((tm, FNET_WIDTH), lambda i: (i % nst, i // nst))
    outs = [(ATT_WIDTH, BF16), (2 * ATT_KV_WIDTH, BF16), (4 * ATT_KV_WIDTH, BF16), (GLA_QK_WIDTH, BF16),
            (GLA_QK_WIDTH, BF16), (GLA_WIDTH, BF16), (GLA_WIDTH, BF16), (2 * GLA_QK_WIDTH, F32)]
    return pl.pallas_call(
        _proj_kernel,
        grid=(n // tm,),
        in_specs=[row(d), const(w), tab, tab, const(qg), const(kg), const(vpat), const(bd), const(gm),
                  const(gb), const(cc), const(sc)],
        out_specs=[row(w_) for w_, _ in outs] + [zspec, zspec],
        out_shape=[jax.ShapeDtypeStruct((n, w_), dt) for w_, dt in outs]
        + [jax.ShapeDtypeStruct((seq, batch * FNET_WIDTH), BF16)] * 2,
        compiler_params=_cparams("parallel"),
        name="in_proj",
    )(h, w, cos, sin, qg, kg, vpat, bd, gm, gb, cc, sc)


ATT_KEY_CHUNK = 512

def _attn_kernel(q_ref, k_ref, v_ref, o_ref):
    s_len = k_ref.shape[1]
    kc = min(ATT_KEY_CHUNK, s_len)
    lane = lax.broadcasted_iota(jnp.int32, (1, LANES), 1)
    low = lane < HEAD_DIM
    nkc = s_len // kc
    heads = [(p, hh) for p in range(2) for hh in range(2)]
    qms = []
    for p, hh in heads:
        qp = q_ref[0, :, p * LANES:(p + 1) * LANES]
        qms.append(jnp.where(low if hh == 0 else jnp.logical_not(low), qp, jnp.zeros_like(qp)))

    def scores(c):
        kblk = k_ref[0, c * kc:(c + 1) * kc, :]
        return [lax.dot_general(qm, kblk, NT_DIMS, preferred_element_type=F32) for qm in qms]

    m = [None] * len(heads)
    acc = [None] * len(heads)
    s_next = scores(0)
    for c in range(nkc):
        s_cur = s_next
        if c + 1 < nkc:
            s_next = scores(c + 1)
        vblk = v_ref[0, c * kc:(c + 1) * kc, :]
        for i, (p, hh) in enumerate(heads):
            mc = jnp.max(s_cur[i], axis=-1, keepdims=True)
            m_new = mc if c == 0 else jnp.maximum(m[i], mc)
            pe = jnp.exp2(s_cur[i] - m_new).astype(BF16)
            pv = jnp.dot(pe, vblk, preferred_element_type=F32)[:, hh * LANES:(hh + 1) * LANES]
            acc[i] = pv if c == 0 else acc[i] * jnp.exp2(m[i] - m_new) + pv
            m[i] = m_new
    for p in range(2):
        outs = []
        for hh in range(2):
            a = acc[2 * p + hh]
            den = a[:, HEAD_DIM:HEAD_DIM + 1] if hh == 0 else a[:, 0:1]
            outs.append(a / den)
        o_ref[0, :, p * LANES:(p + 1) * LANES] = jnp.where(low, outs[0], outs[1]).astype(BF16)


def _attention(q, k2, v4, tq):
    b, s, _ = q.shape
    gw = ATT_WIDTH // ATT_KV_HEADS
    return pl.pallas_call(
        _attn_kernel,
        grid=(b, ATT_KV_HEADS, s // tq),
        in_specs=[pl.BlockSpec((1, tq, gw), lambda bi, g, qi: (bi, qi, g)),
                  pl.BlockSpec((1, s, LANES), lambda bi, g, qi: (bi, 0, g)),
                  pl.BlockSpec((1, s, 2 * LANES), lambda bi, g, qi: (bi, 0, g))],
        out_specs=pl.BlockSpec((1, tq, gw), lambda bi, g, qi: (bi, qi, g)),
        out_shape=jax.ShapeDtypeStruct((b, s, ATT_WIDTH), BF16),
        compiler_params=_cparams("parallel", "parallel", "parallel"),
        name="gqa_attention",
    )(q, k2, v4)


GLA_TILE = 128


def _gla_constants(t):
    nl = int(math.log2(t))
    i = np.arange(t)[:, None]
    m = np.arange(t)[None, :]
    fwd = [m <= i, m > i]
    bwd = [m >= i, m < i]
    mask_f, mask_b = [], []
    for l in range(nl):
        s = t >> l
        half = s // 2
        blk = (i // s) * s
        ref = blk + half - 1
        right = (i - blk) >= half
        wf = np.where(right, (m > ref) & (m <= i), (m > i) & (m <= ref))
        wb = np.where(right, (m > ref) & (m < i), (m >= i) & (m <= ref))
        fwd.append(wf)
        bwd.append(wb)
        same = (i // s) == (m // s)
        mask_f.append(same)
        mask_b.append(same)
    mask_f.append(i == m)
    wst = np.stack([np.concatenate(fwd, 0), np.concatenate(bwd, 0)]).astype(np.float32)
    tile4 = lambda a: np.tile(a.astype(np.float32), (GLA_HEADS, 1))
    mf = np.stack([tile4(a) for a in mask_f])
    mb = np.stack([tile4(a) for a in mask_b])
    return nl, wst, mf, mb


def _gla_kernel(nl, t, q_ref, k_ref, v_ref, la_ref, r_ref, w_ref, mf_ref, mb_ref, bdm_ref, bdn_ref, g_ref,
                o_ref, st_ref, acc_ref):
    s = q_ref.shape[1]
    nt = s // t
    row = lax.broadcasted_iota(jnp.int32, (t, GLA_QK_WIDTH), 0)
    lane_k = lax.broadcasted_iota(jnp.int32, (1, GLA_QK_WIDTH), 1)
    lane_v = lax.broadcasted_iota(jnp.int32, (1, GLA_WIDTH), 1)
    head_k = [(lane_k // GLA_DK) == h for h in range(GLA_HEADS)]
    head_v = [(lane_v // GLA_DV) == h for h in range(GLA_HEADS)]

    def tile_out(t0, d):
        q = q_ref[0, pl.ds(t0, t), :].astype(F32)
        k = k_ref[0, pl.ds(t0, t), :].astype(F32)
        v = v_ref[0, pl.ds(t0, t), :]
        la = la_ref[0, pl.ds(t0, t), d * GLA_QK_WIDTH:(d + 1) * GLA_QK_WIDTH]
        la_hi = la.astype(BF16)
        la_lo = (la - la_hi.astype(F32)).astype(BF16)
        wst = w_ref[d]
        x = jnp.exp(jnp.dot(wst, la_hi, preferred_element_type=F32)
                    + jnp.dot(wst, la_lo, preferred_element_type=F32))
        xq = x[0:t]
        q_st = (q * xq).astype(BF16)
        k_st = (k * x[t:2 * t]).astype(BF16)
        dec = xq[t - 1:t] if d == 0 else xq[0:1]
        st = st_ref[...]
        o_inter = lax.dot_general(q_st, st.astype(BF16), NT_DIMS, preferred_element_type=F32)
        kv = lax.dot_general(v, k_st, TN_DIMS, preferred_element_type=F32)
        st_ref[...] = st * dec + kv * bdm_ref[...]

        sc = jnp.zeros((GLA_HEADS * t, t), F32)
        m_ref = mf_ref if d == 0 else mb_ref
        for l in range(nl):
            half = t >> (l + 1)
            right = ((row // half) % 2) == 1
            qside = right if d == 0 else jnp.logical_not(right)
            g = jnp.where(qside, q, k) * x[(2 + l) * t:(3 + l) * t]
            a = jnp.where(qside, g, 0.0)
            b = jnp.where(qside, 0.0, g).astype(BF16)
            a4 = jnp.concatenate([jnp.where(head_k[h], a, 0.0) for h in range(GLA_HEADS)], axis=0).astype(BF16)
            sc = sc + lax.dot_general(a4, b, NT_DIMS, preferred_element_type=F32) * m_ref[l]
        if d == 0:
            a4 = jnp.concatenate([jnp.where(head_k[h], q, 0.0) for h in range(GLA_HEADS)], axis=0).astype(BF16)
            sc = sc + lax.dot_general(a4, k.astype(BF16), NT_DIMS, preferred_element_type=F32) * m_ref[nl]
        o4 = jnp.dot(sc.astype(BF16), v, preferred_element_type=F32)
        o = o_inter
        for h in range(GLA_HEADS):
            o = o + jnp.where(head_v[h], o4[h * t:(h + 1) * t], 0.0)
        return o

    st_ref[...] = jnp.zeros_like(st_ref)

    def fwd_body(i, c):
        t0 = pl.multiple_of(i * t, t)
        acc_ref[pl.ds(t0, t), :] = tile_out(t0, 0)
        return c

    lax.fori_loop(0, nt, fwd_body, 0)
    st_ref[...] = jnp.zeros_like(st_ref)

    def bwd_body(i, c):
        t0 = pl.multiple_of((nt - 1 - i) * t, t)
        tot = acc_ref[pl.ds(t0, t), :] + tile_out(t0, 1)
        ss = jnp.dot((tot * tot).astype(BF16), bdn_ref[...], preferred_element_type=F32)
        r = r_ref[0, pl.ds(t0, t), :].astype(F32)
        y = tot * lax.rsqrt(ss * (1.0 / GLA_DV) + RMS_EPS) * g_ref[...] * (r / (1.0 + jnp.exp(-r)))
        o_ref[0, pl.ds(t0, t), :] = y.astype(BF16)
        return c

    lax.fori_loop(0, nt, bwd_body, 0)


def _gla(gq, gk, gv, la, gr, wst, mf, mb, bdm, bdn, g, nl, t):
    b, s, _ = gq.shape
    seq = lambda w_: pl.BlockSpec((1, s, w_), lambda bi: (bi, 0, 0))
    const = lambda a: pl.BlockSpec(a.shape, lambda bi: (0,) * a.ndim)
    return pl.pallas_call(
        functools.partial(_gla_kernel, nl, t),
        grid=(b,),
        in_specs=[seq(GLA_QK_WIDTH), seq(GLA_QK_WIDTH), seq(GLA_WIDTH), seq(2 * GLA_QK_WIDTH), seq(GLA_WIDTH),
                  const(wst), const(mf), const(mb), const(bdm), const(bdn), const(g)],
        out_specs=seq(GLA_WIDTH),
        out_shape=jax.ShapeDtypeStruct((b, s, GLA_WIDTH), BF16),
        scratch_shapes=[pltpu.VMEM((GLA_WIDTH, GLA_QK_WIDTH), F32), pltpu.VMEM((s, GLA_WIDTH), F32)],
        compiler_params=_cparams("parallel"),
        name="gla_scan",
    )(gq, gk, gv, la, gr, wst, mf, mb, bdm, bdn, g)


def _dft_kernel(fc_ref, fs_ref, zc_ref, zs_ref, o_ref):
    o_ref[...] = (jnp.dot(fc_ref[...], zc_ref[...], preferred_element_type=F32)
                  + jnp.dot(fs_ref[...], zs_ref[...], preferred_element_type=F32)).astype(BF16)


def _position_dft(fc, fs, zc, zs, tm, tn):
    s, w = zc.shape
    return pl.pallas_call(
        _dft_kernel,
        grid=(w // tn, s // tm),
        in_specs=[pl.BlockSpec((tm, s), lambda j, i: (i, 0)),
                  pl.BlockSpec((tm, s), lambda j, i: (i, 0)),
                  pl.BlockSpec((s, tn), lambda j, i: (0, j)),
                  pl.BlockSpec((s, tn), lambda j, i: (0, j))],
        out_specs=pl.BlockSpec((tm, tn), lambda j, i: (i, j)),
        out_shape=jax.ShapeDtypeStruct((s, w), BF16),
        compiler_params=_cparams("parallel", "parallel"),
        name="position_dft",
    )(fc, fs, zc, zs)


def _outproj_kernel(alpha, n_exp, oatt_ref, ogla_ref, offt_ref, h_ref, wo_ref, g_ref, b_ref,
                    wrh_ref, wrl_ref, rb_ref, us_ref,
                    h1_ref, idx_ref, wts_ref, rank_ref, cnt_ref, carry_ref):
    i = pl.program_id(0)

    @pl.when(i == 0)
    def _():
        carry_ref[...] = jnp.zeros_like(carry_ref)

    acc = jnp.dot(oatt_ref[...], wo_ref[0:ATT_WIDTH, :], preferred_element_type=F32)
    acc = acc + jnp.dot(ogla_ref[...], wo_ref[ATT_WIDTH:ATT_WIDTH + GLA_WIDTH, :], preferred_element_type=F32)
    acc = acc + jnp.dot(offt_ref[...], wo_ref[ATT_WIDTH + GLA_WIDTH:, :], preferred_element_type=F32)
    h1 = _layer_norm_rows(alpha * h_ref[...] + acc, g_ref[...], b_ref[...])
    h1_ref[...] = h1

    hh = h1.astype(BF16)
    hl = (h1 - hh.astype(F32)).astype(BF16)
    wrh = wrh_ref[...]
    logits = (lax.dot_general(wrh, hh, NT_DIMS, preferred_element_type=F32)
              + lax.dot_general(wrh, hl, NT_DIMS, preferred_element_type=F32)
              + lax.dot_general(wrl_ref[...], hh, NT_DIMS, preferred_element_type=F32)) + rb_ref[...]
    tm = logits.shape[1]
    e_iota = lax.broadcasted_iota(jnp.int32, (n_exp, tm), 0)
    cur = logits
    vals, idxs, sels = [], [], []
    for _k in range(TOP_K):
        m = jnp.max(cur, axis=0, keepdims=True)
        ik = jnp.min(jnp.where(cur == m, e_iota, n_exp), axis=0, keepdims=True)
        sel = e_iota == ik
        vals.append(m)
        idxs.append(ik)
        sels.append(sel)
        cur = jnp.where(sel, -jnp.inf, cur)
    ex = [jnp.exp(v - vals[0]) for v in vals]
    den = ex[0] + ex[1] + ex[2] + ex[3]
    zero = jnp.zeros_like(den)
    idx_ref[...] = jnp.concatenate(idxs, axis=0)
    wts_ref[...] = jnp.concatenate([e / den for e in ex] + [zero] * (wts_ref.shape[0] - TOP_K), axis=0)

    onehot = jnp.zeros((n_exp, tm), F32)
    for sel in sels:
        onehot = onehot + sel.astype(F32)
    before = jnp.dot(onehot.astype(BF16), us_ref[...], preferred_element_type=F32) + carry_ref[:, 0:1]
    ranks = [jnp.sum(jnp.where(sel, before, 0.0), axis=0, keepdims=True) for sel in sels]
    rank_ref[...] = jnp.concatenate(ranks, axis=0).astype(jnp.int32)
    carry = carry_ref[...] + jnp.sum(onehot, axis=1, keepdims=True)
    carry_ref[...] = carry
    cnt_ref[...] = carry.astype(jnp.int32)


def _outproj(alpha, o_att, o_gla, o_fft, h, wo, g, b, wrh, wrl, rb, us, seq, tm):
    n, d = h.shape
    n_exp = wrh.shape[0]
    nst = seq // tm
    row = lambda w_: pl.BlockSpec((tm, w_), lambda i: (i, 0))
    const = lambda a: pl.BlockSpec(a.shape, lambda i: (0,) * a.ndim)
    tok = lambda r: pl.BlockSpec((r, tm), lambda i: (0, i))
    return pl.pallas_call(
        functools.partial(_outproj_kernel, alpha, n_exp),
        grid=(n // tm,),
        in_specs=[row(ATT_WIDTH), row(GLA_WIDTH),
                  pl.BlockSpec((tm, FNET_WIDTH), lambda i: (i % nst, i // nst)),
                  row(d), const(wo), const(g), const(b), const(wrh), const(wrl), const(rb), const(us)],
        out_specs=[row(d), tok(TOP_K), tok(2 * TOP_K), tok(TOP_K),
                   pl.BlockSpec((n_exp, LANES), lambda i: (0, 0))],
        out_shape=[jax.ShapeDtypeStruct((n, d), F32),
                   jax.ShapeDtypeStruct((TOP_K, n), jnp.int32),
                   jax.ShapeDtypeStruct((2 * TOP_K, n), F32),
                   jax.ShapeDtypeStruct((TOP_K, n), jnp.int32),
                   jax.ShapeDtypeStruct((n_exp, LANES), jnp.int32)],
        scratch_shapes=[pltpu.VMEM((n_exp, LANES), F32)],
        compiler_params=_cparams("arbitrary"),
        name="out_proj_router",
    )(o_att, o_gla, o_fft, h, wo, g, b, wrh, wrl, rb, us)


DMA_UNROLL = 8
ROW_SUB = 8


def _row_copy(src, dst, sem):
    return pltpu.make_async_copy(src, dst, sem)


def _dispatch_kernel(tb, dest_ref, h_ref, xs_in_ref, xs_ref, buf, sem):
    del xs_in_ref
    buf[...] = h_ref[...].reshape(buf.shape)

    def start(t, c):
        for k in range(TOP_K):
            _row_copy(buf.at[t], xs_ref.at[dest_ref[k, t]], sem).start()
        return c

    lax.fori_loop(0, tb, start, 0, unroll=DMA_UNROLL)

    def wait(t, c):
        for k in range(TOP_K):
            _row_copy(buf.at[0], xs_ref.at[0], sem).wait()
        return c

    lax.fori_loop(0, tb, wait, 0, unroll=DMA_UNROLL)


def _dispatch(dest, h, xs_prev, tb):
    n, d = h.shape
    return pl.pallas_call(
        functools.partial(_dispatch_kernel, tb),
        grid=(n // tb,),
        in_specs=[pl.BlockSpec((TOP_K, tb), lambda i: (0, i), memory_space=pltpu.SMEM),
                  pl.BlockSpec((tb, d), lambda i: (i, 0)),
                  pl.BlockSpec(memory_space=pl.ANY)],
        out_specs=pl.BlockSpec(memory_space=pl.ANY),
        out_shape=jax.ShapeDtypeStruct(xs_prev.shape, F32),
        scratch_shapes=[pltpu.VMEM((tb, ROW_SUB, d // ROW_SUB), F32), pltpu.SemaphoreType.DMA],
        input_output_aliases={2: 0},
        compiler_params=_cparams("arbitrary"),
        name="moe_dispatch",
    )(dest, h, xs_prev)


def _ffn_kernel(te_ref, tv_ref, x_ref, wgu_ref, bgu_ref, wd_ref, bd_ref, y_ref, wgu_bf, wd_bf):
    j = pl.program_id(0)
    valid = tv_ref[j]
    new_expert = jnp.logical_or(j == 0, te_ref[j] != te_ref[jnp.maximum(j - 1, 0)])

    @pl.when(jnp.logical_and(new_expert, valid > 0))
    def _():
        wgu_bf[...] = wgu_ref[0, 0].astype(BF16)
        wd_bf[...] = wd_ref[0, 0].astype(BF16)

    @pl.when(valid > 0)
    def _():
        de, d = wd_bf.shape
        x = x_ref[...].reshape(x_ref.shape[0], d).astype(BF16)
        gu = jnp.dot(x, wgu_bf[...], preferred_element_type=F32) + bgu_ref[0, 0]
        gate = jnp.minimum(gu[:, :de], SWIGLU_LIMIT)
        up = jnp.clip(gu[:, de:], -SWIGLU_LIMIT, SWIGLU_LIMIT)
        hid = (up + 1.0) * (gate / (1.0 + jnp.exp(-SWIGLU_ALPHA * gate)))
        y = jnp.dot(hid.astype(BF16), wd_bf[...], preferred_element_type=F32) + bd_ref[0, 0]
        y_ref[...] = y.reshape(y_ref.shape)

    @pl.when(valid == 0)
    def _():
        y_ref[...] = jnp.zeros_like(y_ref)


def _expert_ffn(layer, tile_expert, tile_valid, xs, wgu, bgu, wd, bd, tmf):
    rows = xs.shape[0]
    depth, n_exp, d, de2 = wgu.shape
    de = de2 // 2
    xspec = pl.BlockSpec((tmf,) + xs.shape[1:], lambda j, te, tv: (j, 0, 0))
    wspec = lambda r, c: pl.BlockSpec((1, 1, r, c), lambda j, te, tv: (layer, te[j], 0, 0))
    grid_spec = pltpu.PrefetchScalarGridSpec(
        num_scalar_prefetch=2,
        grid=(rows // tmf,),
        in_specs=[xspec, wspec(d, de2), wspec(1, de2), wspec(de, d), wspec(1, d)],
        out_specs=xspec,
        scratch_shapes=[pltpu.VMEM((d, de2), BF16), pltpu.VMEM((de, d), BF16)],
    )
    return pl.pallas_call(
        _ffn_kernel,
        grid_spec=grid_spec,
        out_shape=jax.ShapeDtypeStruct(xs.shape, F32),
        compiler_params=_cparams("arbitrary"),
        name="moe_expert_ffn",
    )(tile_expert, tile_valid, xs, wgu, bgu.reshape(depth, n_exp, 1, de2), wd, bd.reshape(depth, n_exp, 1, d))


def _combine_kernel(alpha, tb, nb, dcur_ref, dnxt_ref, w_ref, h1_ref, g_ref, b_ref, y_ref, o_ref, ybuf, sem):
    i = pl.program_id(0)
    slot = i % 2
    nslot = 1 - slot
    d = h1_ref.shape[1]

    def start_block(dref, s):
        def start(t, c):
            for k in range(TOP_K):
                _row_copy(y_ref.at[dref[k, t]], ybuf.at[s, k, t], sem.at[s]).start()
            return c
        lax.fori_loop(0, tb, start, 0, unroll=DMA_UNROLL)

    def wait_block(s):
        def wait(t, c):
            for k in range(TOP_K):
                _row_copy(y_ref.at[0], ybuf.at[s, k, 0], sem.at[s]).wait()
            return c
        lax.fori_loop(0, tb, wait, 0, unroll=DMA_UNROLL)

    @pl.when(i == 0)
    def _():
        start_block(dcur_ref, slot)

    @pl.when(i + 1 < nb)
    def _():
        start_block(dnxt_ref, nslot)

    wait_block(slot)
    wt = jnp.transpose(w_ref[...])
    acc = alpha * h1_ref[...]
    for k in range(TOP_K):
        acc = acc + ybuf[slot, k].reshape(tb, d) * wt[:, k:k + 1]
    o_ref[...] = _layer_norm_rows(acc, g_ref[...], b_ref[...])


def _combine(alpha, dest, wts, h1, g, b, ys, tb):
    n, d = h1.shape
    nb = n // tb
    dspec = lambda f: pl.BlockSpec((TOP_K, tb), f, memory_space=pltpu.SMEM)
    return pl.pallas_call(
        functools.partial(_combine_kernel, alpha, tb, nb),
        grid=(nb,),
        in_specs=[dspec(lambda i: (0, i)),
                  dspec(lambda i: (0, jnp.minimum(i + 1, nb - 1))),
                  pl.BlockSpec((2 * TOP_K, tb), lambda i: (0, i)),
                  pl.BlockSpec((tb, d), lambda i: (i, 0)),
                  pl.BlockSpec((1, d), lambda i: (0, 0)),
                  pl.BlockSpec((1, d), lambda i: (0, 0)),
                  pl.BlockSpec(memory_space=pl.ANY)],
        out_specs=pl.BlockSpec((tb, d), lambda i: (i, 0)),
        out_shape=jax.ShapeDtypeStruct((n, d), F32),
        scratch_shapes=[pltpu.VMEM((2, TOP_K, tb) + ys.shape[1:], F32),
                        pltpu.SemaphoreType.DMA((2,))],
        compiler_params=_cparams("arbitrary"),
        name="moe_combine",
    )(dest, dest, wts, h1, g.reshape(1, d), b.reshape(1, d), ys)


def _rope_tables(seq):
    pos = jnp.arange(seq)
    row_id = (pos // GRID_W).astype(F32)
    col_id = (pos % GRID_W).astype(F32)
    inv_freq = 1.0 / (ROPE_THETA ** (jnp.arange(ROPE_FREQS, dtype=F32) / ROPE_FREQS))
    lane = np.arange(LANES)
    hd = lane % HEAD_DIM
    freq = hd % ROPE_FREQS
    use_col = (hd // (2 * ROPE_FREQS)) == 1
    second = (hd % (2 * ROPE_FREQS)) >= ROPE_FREQS
    ang = jnp.where(use_col[None, :], col_id[:, None], row_id[:, None]) * inv_freq[freq][None, :]
    sign = np.where(second, 1.0, -1.0).astype(np.float32)
    return jnp.cos(ang), jnp.sin(ang) * sign[None, :]


def _dft_tables(seq):
    idx = jnp.arange(seq, dtype=jnp.int32)
    prod = (idx[:, None] * idx[None, :]) % seq
    ang = prod.astype(F32) * (2.0 * math.pi / seq)
    scale = seq ** -0.5
    fc = (jnp.cos(ang) * scale).astype(BF16)
    fs = (jnp.sin(ang) * -scale).astype(BF16)
    c = np.arange(FNET_GROUP_DIM)
    angc = 2.0 * np.pi * ((c[:, None] * c[None, :]) % FNET_GROUP_DIM) / FNET_GROUP_DIM
    eye = np.eye(FNET_GROUPS)
    cc = np.kron(eye, np.cos(angc)) * FNET_GROUP_DIM ** -0.5
    sc = np.kron(eye, np.sin(angc)) * FNET_GROUP_DIM ** -0.5
    return fc, fs, jnp.asarray(cc, BF16), jnp.asarray(sc, BF16)


def _block_ones(width, group):
    g = np.arange(width) // group
    return jnp.asarray((g[:, None] == g[None, :]).astype(np.float32), BF16)


def _pick(limit, n):
    t = min(limit, n)
    while n % t:
        t //= 2
    return t


def kernel(x, ln_in_g, ln_in_b, w_in, att_q_gain, att_k_gain, gla_gate_w, gla_gate_b, gla_norm_g, w_out,
           ln1_g, ln1_b, router_w, router_b, exp_w_gu, exp_b_gu, exp_w_down, exp_b_down, ln2_g, ln2_b):
    batch, seq, d = x.shape
    n = batch * seq
    depth = w_in.shape[0]
    n_exp = router_w.shape[-1]
    alpha = (2.0 * depth) ** 0.25

    tm = _pick(512, seq)
    tq = _pick(256, seq)
    tg = _pick(GLA_TILE, seq)
    tmf = _pick(512, n)
    tb = _pick(256, n)
    n_tiles = (TOP_K * n) // tmf + n_exp
    rows = n_tiles * tmf

    cos, sin = _rope_tables(seq)
    fc, fs, cc, sc = _dft_tables(seq)
    bd_head = _block_ones(LANES, HEAD_DIM)
    bd_gla = _block_ones(GLA_WIDTH, GLA_DV)
    vpat = np.zeros((1, 4 * ATT_KV_WIDTH), np.float32)
    vpat[0, [HEAD_DIM, LANES, 2 * LANES + HEAD_DIM, 3 * LANES]] = 1.0
    vpat = jnp.asarray(vpat)
    nl, wst, mf, mb = _gla_constants(tg)
    wst = jnp.asarray(wst, BF16)
    mf = jnp.asarray(mf)
    mb = jnp.asarray(mb)
    hv = np.arange(GLA_WIDTH) // GLA_DV
    hk = np.arange(GLA_QK_WIDTH) // GLA_DK
    bdm = jnp.asarray((hv[:, None] == hk[None, :]).astype(np.float32))
    us = np.arange(tm)
    us = jnp.asarray((us[:, None] < us[None, :]).astype(np.float32), BF16)

    w_pad = _pad_in_proj(w_in)
    qg = jnp.tile(att_q_gain, (1, LANES // HEAD_DIM)).reshape(depth, 1, LANES)
    kg = jnp.tile(att_k_gain, (1, LANES // HEAD_DIM)).reshape(depth, 1, LANES)
    gm = jnp.zeros((depth, LANES, 2 * GLA_QK_WIDTH), F32)
    gm = gm.at[:, :GLA_GATE_RANK, :GLA_QK_WIDTH].set(gla_gate_w[:, 0])
    gm = gm.at[:, GLA_GATE_RANK:2 * GLA_GATE_RANK, GLA_QK_WIDTH:].set(gla_gate_w[:, 1]).astype(BF16)
    gb = gla_gate_b.reshape(depth, 1, 2 * GLA_QK_WIDTH)
    gn = jnp.tile(gla_norm_g, (1, GLA_HEADS)).reshape(depth, 1, GLA_WIDTH)
    wo = w_out.astype(BF16)
    wr_t = jnp.swapaxes(router_w, 1, 2)
    wrh = wr_t.astype(BF16)
    wrl = (wr_t - wrh.astype(F32)).astype(BF16)
    rb = router_b.reshape(depth, n_exp, 1)

    e_ids = jnp.arange(n_exp, dtype=jnp.int32)
    tile_ids = jnp.arange(n_tiles, dtype=jnp.int32)
    xs = jnp.zeros((rows, ROW_SUB, d // ROW_SUB), F32)

    h = _layer_norm(x.reshape(n, d), ln_in_g, ln_in_b, tm)
    for l in range(depth):
        q, k2, v4, gq, gk, gv, gr, la, zc, zs = _proj(
            h, w_pad[l], cos, sin, qg[l], kg[l], vpat, bd_head, gm[l], gb[l], cc, sc, batch, seq, tm)
        o_att = _attention(q.reshape(batch, seq, -1), k2.reshape(batch, seq, -1), v4.reshape(batch, seq, -1), tq)
        o_gla = _gla(gq.reshape(batch, seq, -1), gk.reshape(batch, seq, -1), gv.reshape(batch, seq, -1),
                     la.reshape(batch, seq, -1), gr.reshape(batch, seq, -1), wst, mf, mb, bdm, bd_gla, gn[l], nl, tg)
        o_fft = _position_dft(fc, fs, zc, zs, tm, _pick(512, batch * FNET_WIDTH))
        h1, idx, wts, rank, cnt = _outproj(
            alpha, o_att.reshape(n, -1), o_gla.reshape(n, -1), o_fft, h, wo[l], ln1_g[l].reshape(1, d),
            ln1_b[l].reshape(1, d), wrh[l], wrl[l], rb[l], us, seq, tm)

        counts = cnt[:, 0]
        tiles_per = (counts + tmf - 1) // tmf
        tile_end = jnp.cumsum(tiles_per)
        offs = (tile_end - tiles_per) * tmf
        chosen = idx[:, :, None] == e_ids[None, None, :]
        dest = rank + jnp.sum(jnp.where(chosen, offs[None, None, :], 0), axis=-1)
        used = tile_end[-1]
        tile_of = jnp.minimum(tile_ids, used - 1)
        tile_expert = jnp.sum((tile_end[None, :] <= tile_of[:, None]).astype(jnp.int32), axis=1)
        own = tile_expert[:, None] == e_ids[None, :]
        group_end = jnp.sum(jnp.where(own, (offs + counts)[None, :], 0), axis=1)
        tile_valid = jnp.where(tile_ids < used, jnp.clip(group_end - tile_ids * tmf, 0, tmf), 0).astype(jnp.int32)

        xs = _dispatch(dest, h1, xs, tb)
        ys = _expert_ffn(l, tile_expert, tile_valid, xs, exp_w_gu, exp_b_gu, exp_w_down, exp_b_down, tmf)
        h = _combine(alpha, dest, wts, h1, ln2_g[l], ln2_b[l], ys, tb)
    return h.reshape(batch, seq, d)
```

```python
import functools
import math

import numpy as np
import jax
import jax.numpy as jnp
from jax import lax
from jax.experimental import pallas as pl
from jax.experimental.pallas import tpu as pltpu

F32 = jnp.float32
BF16 = jnp.bfloat16

GRID_W = 64
HEAD_DIM = 64
ATT_HEADS = 8
ATT_KV_HEADS = 2
ATT_WIDTH = ATT_HEADS * HEAD_DIM
ATT_KV_WIDTH = ATT_KV_HEADS * HEAD_DIM
ROPE_THETA = 10000.0
ROPE_FREQS = HEAD_DIM // 4
GLA_HEADS = 4
GLA_DK = 32
GLA_DV = 64
GLA_QK_WIDTH = GLA_HEADS * GLA_DK
GLA_WIDTH = GLA_HEADS * GLA_DV
GLA_GATE_RANK = 16
GLA_TAU = 16.0
FNET_GROUPS = 4
FNET_GROUP_DIM = 64
FNET_WIDTH = FNET_GROUPS * FNET_GROUP_DIM
TOP_K = 4
SWIGLU_LIMIT = 7.0
SWIGLU_ALPHA = 1.702
LN_EPS = 1e-5
RMS_EPS = 1e-6

LANES = 128
VMEM_LIMIT_BYTES = 56 * 1024 * 1024

NT_DIMS = (((1,), (1,)), ((), ()))
TN_DIMS = (((0,), (0,)), ((), ()))


def _cparams(*sem):
    return pltpu.CompilerParams(dimension_semantics=sem, vmem_limit_bytes=VMEM_LIMIT_BYTES)


def _layer_norm_rows(x, g, b):
    mu = jnp.mean(x, axis=-1, keepdims=True)
    xc = x - mu
    var = jnp.mean(xc * xc, axis=-1, keepdims=True)
    return xc * lax.rsqrt(var + LN_EPS) * g + b


def _ln_kernel(x_ref, g_ref, b_ref, o_ref):
    o_ref[...] = _layer_norm_rows(x_ref[...], g_ref[...], b_ref[...])


def _layer_norm(x, g, b, tm):
    n, d = x.shape
    return pl.pallas_call(
        _ln_kernel,
        grid=(n // tm,),
        in_specs=[pl.BlockSpec((tm, d), lambda i: (i, 0)),
                  pl.BlockSpec((1, d), lambda i: (0, 0)),
                  pl.BlockSpec((1, d), lambda i: (0, 0))],
        out_specs=pl.BlockSpec((tm, d), lambda i: (i, 0)),
        out_shape=jax.ShapeDtypeStruct((n, d), F32),
        compiler_params=_cparams("parallel"),
        name="ln_in",
    )(x, g.reshape(1, d), b.reshape(1, d))


P_Q = 0
P_K = P_Q + ATT_WIDTH
P_V = P_K + 2 * ATT_KV_WIDTH
P_GQ = P_V + ATT_KV_WIDTH
P_GK = P_GQ + GLA_QK_WIDTH
P_GV = P_GK + GLA_QK_WIDTH
P_GR = P_GV + GLA_WIDTH
P_LR = P_GR + GLA_WIDTH
P_FX = P_LR + LANES
P_WIDTH = P_FX + FNET_WIDTH

Q_SCALE = HEAD_DIM ** -0.5 * math.log2(math.e)


def _pad_in_proj(w_in):
    l, d, _ = w_in.shape
    o = 0
    aq = w_in[..., o:o + ATT_WIDTH]; o += ATT_WIDTH
    ak = w_in[..., o:o + ATT_KV_WIDTH]; o += ATT_KV_WIDTH
    av = w_in[..., o:o + ATT_KV_WIDTH]; o += ATT_KV_WIDTH
    gq = w_in[..., o:o + GLA_QK_WIDTH]; o += GLA_QK_WIDTH
    gk = w_in[..., o:o + GLA_QK_WIDTH]; o += GLA_QK_WIDTH
    gv = w_in[..., o:o + GLA_WIDTH]; o += GLA_WIDTH
    gr = w_in[..., o:o + GLA_WIDTH]; o += GLA_WIDTH
    lr = w_in[..., o:o + 2 * GLA_GATE_RANK]; o += 2 * GLA_GATE_RANK
    fx = w_in[..., o:o + FNET_WIDTH]
    k0, k1 = ak[..., :HEAD_DIM], ak[..., HEAD_DIM:]
    zlr = jnp.zeros((l, d, LANES - 2 * GLA_GATE_RANK), w_in.dtype)
    cols = [aq, k0, k0, k1, k1, av, gq, gk, gv, gr, lr, zlr, fx]
    return jnp.concatenate(cols, axis=-1).astype(BF16)


def _proj_kernel(h_ref, w_ref, cos_ref, sin_ref, qg_ref, kg_ref, bd_ref, gm_ref, gb_ref,
                 cc_ref, sc_ref,
                 q_ref, k2_ref, v_ref, gq_ref, gk_ref, gv_ref, gr_ref, la_ref, zc_ref, zs_ref):
    acc = jnp.dot(h_ref[...].astype(BF16), w_ref[...], preferred_element_type=F32)
    cos = cos_ref[...]
    sin = sin_ref[...]
    bd = bd_ref[...]
    lane = lax.broadcasted_iota(jnp.int32, (1, LANES), 1)
    first_half = (lane % (2 * ROPE_FREQS)) < ROPE_FREQS

    def norm_rope(x, gain, scale):
        ss = jnp.dot((x * x).astype(BF16), bd, preferred_element_type=F32)
        xn = x * lax.rsqrt(ss * (1.0 / HEAD_DIM) + RMS_EPS) * gain
        partner = jnp.where(first_half, pltpu.roll(xn, LANES - ROPE_FREQS, 1), pltpu.roll(xn, ROPE_FREQS, 1))
        return (xn * cos + partner * sin) * scale

    for c in range(ATT_WIDTH // LANES):
        x = acc[:, P_Q + c * LANES:P_Q + (c + 1) * LANES]
        q_ref[:, c * LANES:(c + 1) * LANES] = norm_rope(x, qg_ref[...], Q_SCALE).astype(BF16)
    for c in range(2 * ATT_KV_WIDTH // LANES):
        x = acc[:, P_K + c * LANES:P_K + (c + 1) * LANES]
        k2_ref[:, c * LANES:(c + 1) * LANES] = norm_rope(x, kg_ref[...], 1.0).astype(BF16)
    v_ref[...] = acc[:, P_V:P_V + ATT_KV_WIDTH].astype(BF16)

    gq_ref[...] = (acc[:, P_GQ:P_GQ + GLA_QK_WIDTH] * (GLA_DK ** -0.5)).astype(BF16)
    gk_ref[...] = acc[:, P_GK:P_GK + GLA_QK_WIDTH].astype(BF16)
    gv_ref[...] = acc[:, P_GV:P_GV + GLA_WIDTH].astype(BF16)
    gr_ref[...] = acc[:, P_GR:P_GR + GLA_WIDTH].astype(BF16)
    z = jnp.dot(acc[:, P_LR:P_LR + LANES].astype(BF16), gm_ref[...], preferred_element_type=F32) + gb_ref[...]
    la_ref[...] = (jnp.minimum(z, 0.0) - jnp.log(1.0 + jnp.exp(-jnp.abs(z)))) * (1.0 / GLA_TAU)

    fx = acc[:, P_FX:P_FX + FNET_WIDTH].astype(BF16)
    zc_ref[...] = jnp.dot(fx, cc_ref[...], preferred_element_type=F32).astype(BF16)
    zs_ref[...] = jnp.dot(fx, sc_ref[...], preferred_element_type=F32).astype(BF16)


def _proj(h, w, cos, sin, qg, kg, bd, gm, gb, cc, sc, batch, seq, tm):
    n, d = h.shape
    nst = seq // tm
    row = lambda w_: pl.BlockSpec((tm, w_), lambda i: (i, 0))
    const = lambda a: pl.BlockSpec(a.shape, lambda i: (0,) * a.ndim)
    tab = pl.BlockSpec((tm, LANES), lambda i: (i % nst, 0))
    zspec = pl.BlockSpec((tm, FNET_WIDTH), lambda i: (i % nst, i // nst))
    outs = [(ATT_WIDTH, BF16), (2 * ATT_KV_WIDTH, BF16), (ATT_KV_WIDTH, BF16), (GLA_QK_WIDTH, BF16),
            (GLA_QK_WIDTH, BF16), (GLA_WIDTH, BF16), (GLA_WIDTH, BF16), (2 * GLA_QK_WIDTH, F32)]
    return pl.pallas_call(
        _proj_kernel,
        grid=(n // tm,),
        in_specs=[row(d), const(w), tab, tab, const(qg), const(kg), const(bd), const(gm),
                  const(gb), const(cc), const(sc)],
        out_specs=[row(w_) for w_, _ in outs] + [zspec, zspec],
        out_shape=[jax.ShapeDtypeStruct((n, w_), dt) for w_, dt in outs]
        + [jax.ShapeDtypeStruct((seq, batch * FNET_WIDTH), BF16)] * 2,
        compiler_params=_cparams("parallel"),
        name="in_proj",
    )(h, w, cos, sin, qg, kg, bd, gm, gb, cc, sc)


ATT_KEY_CHUNK = 512
ATT_V_ROWS = 80


def _attn_kernel(q_ref, k_ref, vt_ref, o_ref):
    s_len = k_ref.shape[1]
    kc = min(ATT_KEY_CHUNK, s_len)
    nkc = s_len // kc
    lane = lax.broadcasted_iota(jnp.int32, (1, LANES), 1)
    low = lane < HEAD_DIM
    heads = [(p, hh) for p in range(2) for hh in range(2)]
    qms = []
    for p, hh in heads:
        qp = q_ref[0, :, p * LANES:(p + 1) * LANES]
        qms.append(jnp.where(low if hh == 0 else jnp.logical_not(low), qp, jnp.zeros_like(qp)))

    def scores(c):
        kblk = k_ref[0, c * kc:(c + 1) * kc, :]
        return [lax.dot_general(kblk, qm, NT_DIMS, preferred_element_type=F32) for qm in qms]

    m = [None] * len(heads)
    acc = [None] * len(heads)
    s_next = scores(0)
    for c in range(nkc):
        s_cur = s_next
        if c + 1 < nkc:
            s_next = scores(c + 1)
        vt = vt_ref[0, 0, :, c * kc:(c + 1) * kc]
        for i in range(len(heads)):
            mc = jnp.max(s_cur[i], axis=0, keepdims=True)
            m_new = mc if c == 0 else jnp.maximum(m[i], mc)
            pe = jnp.exp2(s_cur[i] - m_new).astype(BF16)
            pv = jnp.dot(vt, pe, preferred_element_type=F32)
            acc[i] = pv if c == 0 else acc[i] * jnp.exp2(m[i] - m_new) + pv
            m[i] = m_new
    outs = [a[0:HEAD_DIM] / a[HEAD_DIM:HEAD_DIM + 1] for a in acc]
    o_ref[0] = jnp.transpose(jnp.concatenate(outs, axis=0)).astype(BF16)


def _attention(q, k2, vt, tq):
    b, s, _ = q.shape
    gw = ATT_WIDTH // ATT_KV_HEADS
    return pl.pallas_call(
        _attn_kernel,
        grid=(b, ATT_KV_HEADS, s // tq),
        in_specs=[pl.BlockSpec((1, tq, gw), lambda bi, g, qi: (bi, qi, g)),
                  pl.BlockSpec((1, s, LANES), lambda bi, g, qi: (bi, 0, g)),
                  pl.BlockSpec((1, 1, ATT_V_ROWS, s), lambda bi, g, qi: (bi, g, 0, 0))],
        out_specs=pl.BlockSpec((1, tq, gw), lambda bi, g, qi: (bi, qi, g)),
        out_shape=jax.ShapeDtypeStruct((b, s, ATT_WIDTH), BF16),
        compiler_params=_cparams("parallel", "parallel", "parallel"),
        name="gqa_attention",
    )(q, k2, vt)


GLA_TILE = 128


def _gla_constants(t):
    nl = int(math.log2(t))
    i = np.arange(t)[:, None]
    m = np.arange(t)[None, :]
    fwd = [m <= i, m > i]
    bwd = [m >= i, m < i]
    mask_f, mask_b = [], []
    for l in range(nl):
        s = t >> l
        half = s // 2
        blk = (i // s) * s
        ref = blk + half - 1
        right = (i - blk) >= half
        wf = np.where(right, (m > ref) & (m <= i), (m > i) & (m <= ref))
        wb = np.where(right, (m > ref) & (m < i), (m >= i) & (m <= ref))
        fwd.append(wf)
        bwd.append(wb)
        same = (i // s) == (m // s)
        mask_f.append(same)
        mask_b.append(same)
    mask_f.append(i == m)
    wst = np.stack([np.concatenate(fwd, 0), np.concatenate(bwd, 0)]).astype(np.float32)
    tile4 = lambda a: np.tile(a.astype(np.float32), (GLA_HEADS, 1))
    mf = np.stack([tile4(a) for a in mask_f])
    mb = np.stack([tile4(a) for a in mask_b])
    return nl, wst, mf, mb


def _gla_kernel(nl, t, q_ref, k_ref, v_ref, la_ref, r_ref, w_ref, mf_ref, mb_ref, bdm_ref, bdn_ref, g_ref,
                o_ref, st_ref, acc_ref):
    s = q_ref.shape[1]
    nt = s // t
    row = lax.broadcasted_iota(jnp.int32, (t, GLA_QK_WIDTH), 0)
    lane_k = lax.broadcasted_iota(jnp.int32, (1, GLA_QK_WIDTH), 1)
    lane_v = lax.broadcasted_iota(jnp.int32, (1, GLA_WIDTH), 1)
    head_k = [(lane_k // GLA_DK) == h for h in range(GLA_HEADS)]
    head_v = [(lane_v // GLA_DV) == h for h in range(GLA_HEADS)]

    dirs = (0, 1)

    def stack_heads(a):
        return jnp.concatenate([jnp.where(head_k[h], a, 0.0) for h in range(GLA_HEADS)], axis=0).astype(BF16)

    def tile_pair(t0s):
        q, k, v, x, o_inter, sc = [], [], [], [], [], []
        for d in dirs:
            t0 = t0s[d]
            q.append(q_ref[0, pl.ds(t0, t), :].astype(F32))
            k.append(k_ref[0, pl.ds(t0, t), :].astype(F32))
            v.append(v_ref[0, pl.ds(t0, t), :])
            la = la_ref[0, pl.ds(t0, t), d * GLA_QK_WIDTH:(d + 1) * GLA_QK_WIDTH]
            la_hi = la.astype(BF16)
            la_lo = (la - la_hi.astype(F32)).astype(BF16)
            wst = w_ref[d]
            x.append(jnp.exp(jnp.dot(wst, la_hi, preferred_element_type=F32)
                             + jnp.dot(wst, la_lo, preferred_element_type=F32)))
        for d in dirs:
            xq = x[d][0:t]
            q_st = (q[d] * xq).astype(BF16)
            k_st = (k[d] * x[d][t:2 * t]).astype(BF16)
            dec = xq[t - 1:t] if d == 0 else xq[0:1]
            st = st_ref[d]
            o_inter.append(lax.dot_general(q_st, st.astype(BF16), NT_DIMS, preferred_element_type=F32))
            kv = lax.dot_general(v[d], k_st, TN_DIMS, preferred_element_type=F32)
            st_ref[d] = st * dec + kv * bdm_ref[...]
            sc.append(jnp.zeros((GLA_HEADS * t, t), F32))
        for l in range(nl):
            half = t >> (l + 1)
            right = ((row // half) % 2) == 1
            for d in dirs:
                qside = right if d == 0 else jnp.logical_not(right)
                g = jnp.where(qside, q[d], k[d]) * x[d][(2 + l) * t:(3 + l) * t]
                a4 = stack_heads(jnp.where(qside, g, 0.0))
                b = jnp.where(qside, 0.0, g).astype(BF16)
                m_ref = mf_ref if d == 0 else mb_ref
                sc[d] = sc[d] + lax.dot_general(a4, b, NT_DIMS, preferred_element_type=F32) * m_ref[l]
        sc[0] = sc[0] + lax.dot_general(stack_heads(q[0]), k[0].astype(BF16), NT_DIMS,
                                        preferred_element_type=F32) * mf_ref[nl]
        outs = []
        for d in dirs:
            o4 = jnp.dot(sc[d].astype(BF16), v[d], preferred_element_type=F32)
            o = o_inter[d]
            for h in range(GLA_HEADS):
                o = o + jnp.where(head_v[h], o4[h * t:(h + 1) * t], 0.0)
            outs.append(o)
        return outs

    def finish(t0, tot):
        ss = jnp.dot((tot * tot).astype(BF16), bdn_ref[...], preferred_element_type=F32)
        r = r_ref[0, pl.ds(t0, t), :].astype(F32)
        y = tot * lax.rsqrt(ss * (1.0 / GLA_DV) + RMS_EPS) * g_ref[...] * (r / (1.0 + jnp.exp(-r)))
        o_ref[0, pl.ds(t0, t), :] = y.astype(BF16)

    def tiles(i):
        return pl.multiple_of(i * t, t), pl.multiple_of((nt - 1 - i) * t, t)

    st_ref[...] = jnp.zeros_like(st_ref)

    def park(i, c):
        t0s = tiles(i)
        for t0, o in zip(t0s, tile_pair(t0s)):
            acc_ref[pl.ds(t0, t), :] = o
        return c

    def complete(i, c):
        t0s = tiles(i)
        for t0, o in zip(t0s, tile_pair(t0s)):
            finish(t0, acc_ref[pl.ds(t0, t), :] + o)
        return c

    lax.fori_loop(0, nt // 2, park, 0)
    lax.fori_loop(nt // 2, nt, complete, 0)


def _gla(gq, gk, gv, la, gr, wst, mf, mb, bdm, bdn, g, nl, t):
    b, s, _ = gq.shape
    seq = lambda w_: pl.BlockSpec((1, s, w_), lambda bi: (bi, 0, 0))
    const = lambda a: pl.BlockSpec(a.shape, lambda bi: (0,) * a.ndim)
    return pl.pallas_call(
        functools.partial(_gla_kernel, nl, t),
        grid=(b,),
        in_specs=[seq(GLA_QK_WIDTH), seq(GLA_QK_WIDTH), seq(GLA_WIDTH), seq(2 * GLA_QK_WIDTH), seq(GLA_WIDTH),
                  const(wst), const(mf), const(mb), const(bdm), const(bdn), const(g)],
        out_specs=seq(GLA_WIDTH),
        out_shape=jax.ShapeDtypeStruct((b, s, GLA_WIDTH), BF16),
        scratch_shapes=[pltpu.VMEM((2, GLA_WIDTH, GLA_QK_WIDTH), F32), pltpu.VMEM((s, GLA_WIDTH), F32)],
        compiler_params=_cparams("parallel"),
        name="gla_scan",
    )(gq, gk, gv, la, gr, wst, mf, mb, bdm, bdn, g)


def _dft_kernel(fc_ref, fs_ref, zc_ref, zs_ref, o_ref):
    o_ref[...] = (jnp.dot(fc_ref[...], zc_ref[...], preferred_element_type=F32)
                  + jnp.dot(fs_ref[...], zs_ref[...], preferred_element_type=F32)).astype(BF16)


def _position_dft(fc, fs, zc, zs, tm, tn):
    s, w = zc.shape
    return pl.pallas_call(
        _dft_kernel,
        grid=(w // tn, s // tm),
        in_specs=[pl.BlockSpec((tm, s), lambda j, i: (i, 0)),
                  pl.BlockSpec((tm, s), lambda j, i: (i, 0)),
                  pl.BlockSpec((s, tn), lambda j, i: (0, j)),
                  pl.BlockSpec((s, tn), lambda j, i: (0, j))],
        out_specs=pl.BlockSpec((tm, tn), lambda j, i: (i, j)),
        out_shape=jax.ShapeDtypeStruct((s, w), BF16),
        compiler_params=_cparams("parallel", "parallel"),
        name="position_dft",
    )(fc, fs, zc, zs)


def _outproj_kernel(alpha, n_exp, oatt_ref, ogla_ref, offt_ref, h_ref, wo_ref, g_ref, b_ref,
                    wrh_ref, wrl_ref, rb_ref, us_ref,
                    h1_ref, idx_ref, wts_ref, rank_ref, cnt_ref, carry_ref):
    i = pl.program_id(0)

    @pl.when(i == 0)
    def _():
        carry_ref[...] = jnp.zeros_like(carry_ref)

    acc = jnp.dot(oatt_ref[...], wo_ref[0:ATT_WIDTH, :], preferred_element_type=F32)
    acc = acc + jnp.dot(ogla_ref[...], wo_ref[ATT_WIDTH:ATT_WIDTH + GLA_WIDTH, :], preferred_element_type=F32)
    acc = acc + jnp.dot(offt_ref[...], wo_ref[ATT_WIDTH + GLA_WIDTH:, :], preferred_element_type=F32)
    h1 = _layer_norm_rows(alpha * h_ref[...] + acc, g_ref[...], b_ref[...])
    h1_ref[...] = h1

    hh = h1.astype(BF16)
    hl = (h1 - hh.astype(F32)).astype(BF16)
    wrh = wrh_ref[...]
    logits = (lax.dot_general(wrh, hh, NT_DIMS, preferred_element_type=F32)
              + lax.dot_general(wrh, hl, NT_DIMS, preferred_element_type=F32)
              + lax.dot_general(wrl_ref[...], hh, NT_DIMS, preferred_element_type=F32)) + rb_ref[...]
    tm = logits.shape[1]
    e_iota = lax.broadcasted_iota(jnp.int32, (n_exp, tm), 0)
    cur = logits
    vals, idxs, sels = [], [], []
    for _k in range(TOP_K):
        m = jnp.max(cur, axis=0, keepdims=True)
        ik = jnp.min(jnp.where(cur == m, e_iota, n_exp), axis=0, keepdims=True)
        sel = e_iota == ik
        vals.append(m)
        idxs.append(ik)
        sels.append(sel)
        cur = jnp.where(sel, -jnp.inf, cur)
    ex = [jnp.exp(v - vals[0]) for v in vals]
    den = ex[0] + ex[1] + ex[2] + ex[3]
    zero = jnp.zeros_like(den)
    idx_ref[...] = jnp.concatenate(idxs, axis=0)
    wts_ref[...] = jnp.concatenate([e / den for e in ex] + [zero] * (wts_ref.shape[0] - TOP_K), axis=0)

    onehot = jnp.zeros((n_exp, tm), F32)
    for sel in sels:
        onehot = onehot + sel.astype(F32)
    before = jnp.dot(onehot.astype(BF16), us_ref[...], preferred_element_type=F32) + carry_ref[:, 0:1]
    ranks = [jnp.sum(jnp.where(sel, before, 0.0), axis=0, keepdims=True) for sel in sels]
    rank_ref[...] = jnp.concatenate(ranks, axis=0).astype(jnp.int32)
    carry = carry_ref[...] + jnp.sum(onehot, axis=1, keepdims=True)
    carry_ref[...] = carry
    cnt_ref[...] = carry.astype(jnp.int32)


def _outproj(alpha, o_att, o_gla, o_fft, h, wo, g, b, wrh, wrl, rb, us, seq, tm):
    n, d = h.shape
    n_exp = wrh.shape[0]
    nst = seq // tm
    row = lambda w_: pl.BlockSpec((tm, w_), lambda i: (i, 0))
    const = lambda a: pl.BlockSpec(a.shape, lambda i: (0,) * a.ndim)
    tok = lambda r: pl.BlockSpec((r, tm), lambda i: (0, i))
    return pl.pallas_call(
        functools.partial(_outproj_kernel, alpha, n_exp),
        grid=(n // tm,),
        in_specs=[row(ATT_WIDTH), row(GLA_WIDTH),
                  pl.BlockSpec((tm, FNET_WIDTH), lambda i: (i % nst, i // nst)),
                  row(d), const(wo), const(g), const(b), const(wrh), const(wrl), const(rb), const(us)],
        out_specs=[row(d), tok(TOP_K), tok(2 * TOP_K), tok(TOP_K),
                   pl.BlockSpec((n_exp, LANES), lambda i: (0, 0))],
        out_shape=[jax.ShapeDtypeStruct((n, d), F32),
                   jax.ShapeDtypeStruct((TOP_K, n), jnp.int32),
                   jax.ShapeDtypeStruct((2 * TOP_K, n), F32),
                   jax.ShapeDtypeStruct((TOP_K, n), jnp.int32),
                   jax.ShapeDtypeStruct((n_exp, LANES), jnp.int32)],
        scratch_shapes=[pltpu.VMEM((n_exp, LANES), F32)],
        compiler_params=_cparams("arbitrary"),
        name="out_proj_router",
    )(o_att, o_gla, o_fft, h, wo, g, b, wrh, wrl, rb, us)


DMA_UNROLL = 8
ROW_SUB = 8


def _row_copy(src, dst, sem):
    return pltpu.make_async_copy(src, dst, sem)


def _dispatch_kernel(tb, nb, dest_ref, h_ref, xs_in_ref, xs_ref, buf, sem):
    del xs_in_ref
    i = pl.program_id(0)
    slot = i % 2
    buf[slot] = h_ref[...].reshape(buf.shape[1:])

    def start(t, c):
        for k in range(TOP_K):
            _row_copy(buf.at[slot, t], xs_ref.at[dest_ref[k, t]], sem.at[slot]).start(priority=k % 2)
        return c

    lax.fori_loop(0, tb, start, 0, unroll=DMA_UNROLL)

    def wait_block(s):
        def wait(t, c):
            for k in range(TOP_K):
                _row_copy(buf.at[s, 0], xs_ref.at[0], sem.at[s]).wait()
            return c
        lax.fori_loop(0, tb, wait, 0, unroll=DMA_UNROLL)

    @pl.when(i > 0)
    def _():
        wait_block(1 - slot)

    @pl.when(i == nb - 1)
    def _():
        wait_block(slot)


def _dispatch(dest, h, xs_prev, tb):
    n, d = h.shape
    return pl.pallas_call(
        functools.partial(_dispatch_kernel, tb, n // tb),
        grid=(n // tb,),
        in_specs=[pl.BlockSpec((TOP_K, tb), lambda i: (0, i), memory_space=pltpu.SMEM),
                  pl.BlockSpec((tb, d), lambda i: (i, 0)),
                  pl.BlockSpec(memory_space=pl.ANY)],
        out_specs=pl.BlockSpec(memory_space=pl.ANY),
        out_shape=jax.ShapeDtypeStruct(xs_prev.shape, F32),
        scratch_shapes=[pltpu.VMEM((2, tb, ROW_SUB, d // ROW_SUB), F32), pltpu.SemaphoreType.DMA((2,))],
        input_output_aliases={2: 0},
        compiler_params=_cparams("arbitrary"),
        name="moe_dispatch",
    )(dest, h, xs_prev)


def _ffn_kernel(te_ref, tv_ref, x_ref, wgu_ref, bgu_ref, wd_ref, bd_ref, y_ref, wgu_bf, wd_bf):
    j = pl.program_id(0)
    valid = tv_ref[j]
    new_expert = jnp.logical_or(j == 0, te_ref[j] != te_ref[jnp.maximum(j - 1, 0)])

    @pl.when(jnp.logical_and(new_expert, valid > 0))
    def _():
        wgu_bf[...] = wgu_ref[0, 0].astype(BF16)
        wd_bf[...] = wd_ref[0, 0].astype(BF16)

    @pl.when(valid > 0)
    def _():
        de, d = wd_bf.shape
        x = x_ref[...].reshape(x_ref.shape[0], d).astype(BF16)
        gu = jnp.dot(x, wgu_bf[...], preferred_element_type=F32) + bgu_ref[0, 0]
        gate = jnp.minimum(gu[:, :de], SWIGLU_LIMIT)
        up = jnp.clip(gu[:, de:], -SWIGLU_LIMIT, SWIGLU_LIMIT)
        hid = (up + 1.0) * (gate / (1.0 + jnp.exp(-SWIGLU_ALPHA * gate)))
        y = jnp.dot(hid.astype(BF16), wd_bf[...], preferred_element_type=F32) + bd_ref[0, 0]
        y_ref[...] = y.reshape(y_ref.shape)

    @pl.when(valid == 0)
    def _():
        y_ref[...] = jnp.zeros_like(y_ref)


def _expert_ffn(layer, tile_expert, tile_valid, xs, wgu, bgu, wd, bd, tmf):
    rows = xs.shape[0]
    depth, n_exp, d, de2 = wgu.shape
    de = de2 // 2
    xspec = pl.BlockSpec((tmf,) + xs.shape[1:], lambda j, te, tv: (j, 0, 0))
    wspec = lambda r, c: pl.BlockSpec((1, 1, r, c), lambda j, te, tv: (layer, te[j], 0, 0))
    grid_spec = pltpu.PrefetchScalarGridSpec(
        num_scalar_prefetch=2,
        grid=(rows // tmf,),
        in_specs=[xspec, wspec(d, de2), wspec(1, de2), wspec(de, d), wspec(1, d)],
        out_specs=xspec,
        scratch_shapes=[pltpu.VMEM((d, de2), BF16), pltpu.VMEM((de, d), BF16)],
    )
    return pl.pallas_call(
        _ffn_kernel,
        grid_spec=grid_spec,
        out_shape=jax.ShapeDtypeStruct(xs.shape, F32),
        compiler_params=_cparams("arbitrary"),
        name="moe_expert_ffn",
    )(tile_expert, tile_valid, xs, wgu, bgu.reshape(depth, n_exp, 1, de2), wd, bd.reshape(depth, n_exp, 1, d))


def _combine_kernel(alpha, tb, nb, dcur_ref, dnxt_ref, w_ref, h1_ref, g_ref, b_ref, y_ref, o_ref, ybuf, sem):
    i = pl.program_id(0)
    slot = i % 2
    nslot = 1 - slot
    d = h1_ref.shape[1]

    def start_block(dref, s):
        def start(t, c):
            for k in range(TOP_K):
                _row_copy(y_ref.at[dref[k, t]], ybuf.at[s, k, t], sem.at[s]).start(priority=k % 2)
            return c
        lax.fori_loop(0, tb, start, 0, unroll=DMA_UNROLL)

    def wait_block(s):
        def wait(t, c):
            for k in range(TOP_K):
                _row_copy(y_ref.at[0], ybuf.at[s, k, 0], sem.at[s]).wait()
            return c
        lax.fori_loop(0, tb, wait, 0, unroll=DMA_UNROLL)

    @pl.when(i == 0)
    def _():
        start_block(dcur_ref, slot)

    @pl.when(i + 1 < nb)
    def _():
        start_block(dnxt_ref, nslot)

    wait_block(slot)
    wt = jnp.transpose(w_ref[...])
    acc = alpha * h1_ref[...]
    for k in range(TOP_K):
        acc = acc + ybuf[slot, k].reshape(tb, d) * wt[:, k:k + 1]
    o_ref[...] = _layer_norm_rows(acc, g_ref[...], b_ref[...])


def _combine(alpha, dest, wts, h1, g, b, ys, tb):
    n, d = h1.shape
    nb = n // tb
    dspec = lambda f: pl.BlockSpec((TOP_K, tb), f, memory_space=pltpu.SMEM)
    return pl.pallas_call(
        functools.partial(_combine_kernel, alpha, tb, nb),
        grid=(nb,),
        in_specs=[dspec(lambda i: (0, i)),
                  dspec(lambda i: (0, jnp.minimum(i + 1, nb - 1))),
                  pl.BlockSpec((2 * TOP_K, tb), lambda i: (0, i)),
                  pl.BlockSpec((tb, d), lambda i: (i, 0)),
                  pl.BlockSpec((1, d), lambda i: (0, 0)),
                  pl.BlockSpec((1, d), lambda i: (0, 0)),
                  pl.BlockSpec(memory_space=pl.ANY)],
        out_specs=pl.BlockSpec((tb, d), lambda i: (i, 0)),
        out_shape=jax.ShapeDtypeStruct((n, d), F32),
        scratch_shapes=[pltpu.VMEM((2, TOP_K, tb) + ys.shape[1:], F32),
                        pltpu.SemaphoreType.DMA((2,))],
        compiler_params=_cparams("arbitrary"),
        name="moe_combine",
    )(dest, dest, wts, h1, g.reshape(1, d), b.reshape(1, d), ys)


def _rope_tables(seq):
    pos = jnp.arange(seq)
    row_id = (pos // GRID_W).astype(F32)
    col_id = (pos % GRID_W).astype(F32)
    inv_freq = 1.0 / (ROPE_THETA ** (jnp.arange(ROPE_FREQS, dtype=F32) / ROPE_FREQS))
    lane = np.arange(LANES)
    hd = lane % HEAD_DIM
    freq = hd % ROPE_FREQS
    use_col = (hd // (2 * ROPE_FREQS)) == 1
    second = (hd % (2 * ROPE_FREQS)) >= ROPE_FREQS
    ang = jnp.where(use_col[None, :], col_id[:, None], row_id[:, None]) * inv_freq[freq][None, :]
    sign = np.where(second, 1.0, -1.0).astype(np.float32)
    return jnp.cos(ang), jnp.sin(ang) * sign[None, :]


def _dft_tables(seq):
    idx = jnp.arange(seq, dtype=jnp.int32)
    prod = (idx[:, None] * idx[None, :]) % seq
    ang = prod.astype(F32) * (2.0 * math.pi / seq)
    scale = seq ** -0.5
    fc = (jnp.cos(ang) * scale).astype(BF16)
    fs = (jnp.sin(ang) * -scale).astype(BF16)
    c = np.arange(FNET_GROUP_DIM)
    angc = 2.0 * np.pi * ((c[:, None] * c[None, :]) % FNET_GROUP_DIM) / FNET_GROUP_DIM
    eye = np.eye(FNET_GROUPS)
    cc = np.kron(eye, np.cos(angc)) * FNET_GROUP_DIM ** -0.5
    sc = np.kron(eye, np.sin(angc)) * FNET_GROUP_DIM ** -0.5
    return fc, fs, jnp.asarray(cc, BF16), jnp.asarray(sc, BF16)


def _block_ones(width, group):
    g = np.arange(width) // group
    return jnp.asarray((g[:, None] == g[None, :]).astype(np.float32), BF16)


def _pick(limit, n):
    t = min(limit, n)
    while n % t:
        t //= 2
    return t


def kernel(x, ln_in_g, ln_in_b, w_in, att_q_gain, att_k_gain, gla_gate_w, gla_gate_b, gla_norm_g, w_out,
           ln1_g, ln1_b, router_w, router_b, exp_w_gu, exp_b_gu, exp_w_down, exp_b_down, ln2_g, ln2_b):
    batch, seq, d = x.shape
    n = batch * seq
    depth = w_in.shape[0]
    n_exp = router_w.shape[-1]
    alpha = (2.0 * depth) ** 0.25

    tm = _pick(512, seq)
    tq = _pick(256, seq)
    tg = _pick(GLA_TILE, seq // 2)
    tmf = _pick(512, n)
    tb = _pick(256, n)
    n_tiles = (TOP_K * n) // tmf + n_exp
    rows = n_tiles * tmf

    cos, sin = _rope_tables(seq)
    fc, fs, cc, sc = _dft_tables(seq)
    bd_head = _block_ones(LANES, HEAD_DIM)
    bd_gla = _block_ones(GLA_WIDTH, GLA_DV)
    vt_pad = jnp.zeros((batch, ATT_KV_HEADS, ATT_V_ROWS - HEAD_DIM, seq), BF16).at[:, :, 0, :].set(1.0)
    nl, wst, mf, mb = _gla_constants(tg)
    wst = jnp.asarray(wst, BF16)
    mf = jnp.asarray(mf)
    mb = jnp.asarray(mb)
    hv = np.arange(GLA_WIDTH) // GLA_DV
    hk = np.arange(GLA_QK_WIDTH) // GLA_DK
    bdm = jnp.asarray((hv[:, None] == hk[None, :]).astype(np.float32))
    us = np.arange(tm)
    us = jnp.asarray((us[:, None] < us[None, :]).astype(np.float32), BF16)

    w_pad = _pad_in_proj(w_in)
    qg = jnp.tile(att_q_gain, (1, LANES // HEAD_DIM)).reshape(depth, 1, LANES)
    kg = jnp.tile(att_k_gain, (1, LANES // HEAD_DIM)).reshape(depth, 1, LANES)
    gm = jnp.zeros((depth, LANES, 2 * GLA_QK_WIDTH), F32)
    gm = gm.at[:, :GLA_GATE_RANK, :GLA_QK_WIDTH].set(gla_gate_w[:, 0])
    gm = gm.at[:, GLA_GATE_RANK:2 * GLA_GATE_RANK, GLA_QK_WIDTH:].set(gla_gate_w[:, 1]).astype(BF16)
    gb = gla_gate_b.reshape(depth, 1, 2 * GLA_QK_WIDTH)
    gn = jnp.tile(gla_norm_g, (1, GLA_HEADS)).reshape(depth, 1, GLA_WIDTH)
    wo = w_out.astype(BF16)
    wr_t = jnp.swapaxes(router_w, 1, 2)
    wrh = wr_t.astype(BF16)
    wrl = (wr_t - wrh.astype(F32)).astype(BF16)
    rb = router_b.reshape(depth, n_exp, 1)

    e_ids = jnp.arange(n_exp, dtype=jnp.int32)
    tile_ids = jnp.arange(n_tiles, dtype=jnp.int32)
    xs = jnp.zeros((rows, ROW_SUB, d // ROW_SUB), F32)

    h = _layer_norm(x.reshape(n, d), ln_in_g, ln_in_b, tm)
    for l in range(depth):
        q, k2, av, gq, gk, gv, gr, la, zc, zs = _proj(
            h, w_pad[l], cos, sin, qg[l], kg[l], bd_head, gm[l], gb[l], cc, sc, batch, seq, tm)
        vt = jnp.transpose(av.reshape(batch, seq, ATT_KV_HEADS, HEAD_DIM), (0, 2, 3, 1))
        vt = jnp.concatenate([vt, vt_pad], axis=2)
        o_att = _attention(q.reshape(batch, seq, -1), k2.reshape(batch, seq, -1), vt, tq)
        o_gla = _gla(gq.reshape(batch, seq, -1), gk.reshape(batch, seq, -1), gv.reshape(batch, seq, -1),
                     la.reshape(batch, seq, -1), gr.reshape(batch, seq, -1), wst, mf, mb, bdm, bd_gla, gn[l], nl, tg)
        o_fft = _position_dft(fc, fs, zc, zs, tm, _pick(512, batch * FNET_WIDTH))
        h1, idx, wts, rank, cnt = _outproj(
            alpha, o_att.reshape(n, -1), o_gla.reshape(n, -1), o_fft, h, wo[l], ln1_g[l].reshape(1, d),
            ln1_b[l].reshape(1, d), wrh[l], wrl[l], rb[l], us, seq, tm)

        counts = cnt[:, 0]
        tiles_per = (counts + tmf - 1) // tmf
        tile_end = jnp.cumsum(tiles_per)
        offs = (tile_end - tiles_per) * tmf
        chosen = idx[:, :, None] == e_ids[None, None, :]
        dest = rank + jnp.sum(jnp.where(chosen, offs[None, None, :], 0), axis=-1)
        used = tile_end[-1]
        tile_of = jnp.minimum(tile_ids, used - 1)
        tile_expert = jnp.sum((tile_end[None, :] <= tile_of[:, None]).astype(jnp.int32), axis=1)
        own = tile_expert[:, None] == e_ids[None, :]
        group_end = jnp.sum(jnp.where(own, (offs + counts)[None, :], 0), axis=1)
        tile_valid = jnp.where(tile_ids < used, jnp.clip(group_end - tile_ids * tmf, 0, tmf), 0).astype(jnp.int32)

        xs = _dispatch(dest, h1, xs, tb)
        ys = _expert_ffn(l, tile_expert, tile_valid, xs, exp_w_gu, exp_b_gu, exp_w_down, exp_b_down, tmf)
        h = _combine(alpha, dest, wts, h1, ln2_g[l], ln2_b[l], ys, tb)
    return h.reshape(batch, seq, d)
```

```python
import functools
import math

import numpy as np
import jax
import jax.numpy as jnp
from jax import lax
from jax.experimental import pallas as pl
from jax.experimental.pallas import tpu as pltpu

F32 = jnp.float32
BF16 = jnp.bfloat16

GRID_W = 64
HEAD_DIM = 64
ATT_HEADS = 8
ATT_KV_HEADS = 2
ATT_WIDTH = ATT_HEADS * HEAD_DIM
ATT_KV_WIDTH = ATT_KV_HEADS * HEAD_DIM
ROPE_THETA = 10000.0
ROPE_FREQS = HEAD_DIM // 4
GLA_HEADS = 4
GLA_DK = 32
GLA_DV = 64
GLA_QK_WIDTH = GLA_HEADS * GLA_DK
GLA_WIDTH = GLA_HEADS * GLA_DV
GLA_GATE_RANK = 16
GLA_TAU = 16.0
FNET_GROUPS = 4
FNET_GROUP_DIM = 64
FNET_WIDTH = FNET_GROUPS * FNET_GROUP_DIM
TOP_K = 4
SWIGLU_LIMIT = 7.0
SWIGLU_ALPHA = 1.702
LN_EPS = 1e-5
RMS_EPS = 1e-6

LANES = 128
VMEM_LIMIT_BYTES = 56 * 1024 * 1024

NT_DIMS = (((1,), (1,)), ((), ()))
TN_DIMS = (((0,), (0,)), ((), ()))


def _cparams(*sem):
    return pltpu.CompilerParams(dimension_semantics=sem, vmem_limit_bytes=VMEM_LIMIT_BYTES)


def _layer_norm_rows(x, g, b):
    mu = jnp.mean(x, axis=-1, keepdims=True)
    xc = x - mu
    var = jnp.mean(xc * xc, axis=-1, keepdims=True)
    return xc * lax.rsqrt(var + LN_EPS) * g + b


def _ln_kernel(x_ref, g_ref, b_ref, o_ref):
    o_ref[...] = _layer_norm_rows(x_ref[...], g_ref[...], b_ref[...])


def _layer_norm(x, g, b, tm):
    n, d = x.shape
    return pl.pallas_call(
        _ln_kernel,
        grid=(n // tm,),
        in_specs=[pl.BlockSpec((tm, d), lambda i: (i, 0)),
                  pl.BlockSpec((1, d), lambda i: (0, 0)),
                  pl.BlockSpec((1, d), lambda i: (0, 0))],
        out_specs=pl.BlockSpec((tm, d), lambda i: (i, 0)),
        out_shape=jax.ShapeDtypeStruct((n, d), F32),
        compiler_params=_cparams("parallel"),
        name="ln_in",
    )(x, g.reshape(1, d), b.reshape(1, d))


P_Q = 0
P_K = P_Q + ATT_WIDTH
P_V = P_K + 2 * ATT_KV_WIDTH
P_GQ = P_V + ATT_KV_WIDTH
P_GK = P_GQ + GLA_QK_WIDTH
P_GV = P_GK + GLA_QK_WIDTH
P_GR = P_GV + GLA_WIDTH
P_LR = P_GR + GLA_WIDTH
P_FX = P_LR + LANES
P_WIDTH = P_FX + FNET_WIDTH

Q_SCALE = HEAD_DIM ** -0.5 * math.log2(math.e)


def _pad_in_proj(w_in):
    l, d, _ = w_in.shape
    o = 0
    aq = w_in[..., o:o + ATT_WIDTH]; o += ATT_WIDTH
    ak = w_in[..., o:o + ATT_KV_WIDTH]; o += ATT_KV_WIDTH
    av = w_in[..., o:o + ATT_KV_WIDTH]; o += ATT_KV_WIDTH
    gq = w_in[..., o:o + GLA_QK_WIDTH]; o += GLA_QK_WIDTH
    gk = w_in[..., o:o + GLA_QK_WIDTH]; o += GLA_QK_WIDTH
    gv = w_in[..., o:o + GLA_WIDTH]; o += GLA_WIDTH
    gr = w_in[..., o:o + GLA_WIDTH]; o += GLA_WIDTH
    lr = w_in[..., o:o + 2 * GLA_GATE_RANK]; o += 2 * GLA_GATE_RANK
    fx = w_in[..., o:o + FNET_WIDTH]
    k0, k1 = ak[..., :HEAD_DIM], ak[..., HEAD_DIM:]
    zlr = jnp.zeros((l, d, LANES - 2 * GLA_GATE_RANK), w_in.dtype)
    cols = [aq, k0, k0, k1, k1, av, gq, gk, gv, gr, lr, zlr, fx]
    return jnp.concatenate(cols, axis=-1).astype(BF16)


def _proj_kernel(h_ref, w_ref, cos_ref, sin_ref, qg_ref, kg_ref, bd_ref, gm_ref, gb_ref,
                 cc_ref, sc_ref,
                 q_ref, k2_ref, v_ref, gq_ref, gk_ref, gv_ref, gr_ref, la_ref, zc_ref, zs_ref):
    acc = jnp.dot(h_ref[...].astype(BF16), w_ref[...], preferred_element_type=F32)
    cos = cos_ref[...]
    sin = sin_ref[...]
    bd = bd_ref[...]
    lane = lax.broadcasted_iota(jnp.int32, (1, LANES), 1)
    first_half = (lane % (2 * ROPE_FREQS)) < ROPE_FREQS

    def norm_rope(x, gain, scale):
        ss = jnp.dot((x * x).astype(BF16), bd, preferred_element_type=F32)
        xn = x * lax.rsqrt(ss * (1.0 / HEAD_DIM) + RMS_EPS) * gain
        partner = jnp.where(first_half, pltpu.roll(xn, LANES - ROPE_FREQS, 1), pltpu.roll(xn, ROPE_FREQS, 1))
        return (xn * cos + partner * sin) * scale

    for c in range(ATT_WIDTH // LANES):
        x = acc[:, P_Q + c * LANES:P_Q + (c + 1) * LANES]
        q_ref[:, c * LANES:(c + 1) * LANES] = norm_rope(x, qg_ref[...], Q_SCALE).astype(BF16)
    for c in range(2 * ATT_KV_WIDTH // LANES):
        x = acc[:, P_K + c * LANES:P_K + (c + 1) * LANES]
        k2_ref[:, c * LANES:(c + 1) * LANES] = norm_rope(x, kg_ref[...], 1.0).astype(BF16)
    v_ref[...] = acc[:, P_V:P_V + ATT_KV_WIDTH].astype(BF16)

    gq_ref[...] = (acc[:, P_GQ:P_GQ + GLA_QK_WIDTH] * (GLA_DK ** -0.5)).astype(BF16)
    gk_ref[...] = acc[:, P_GK:P_GK + GLA_QK_WIDTH].astype(BF16)
    gv_ref[...] = acc[:, P_GV:P_GV + GLA_WIDTH].astype(BF16)
    gr_ref[...] = acc[:, P_GR:P_GR + GLA_WIDTH].astype(BF16)
    z = jnp.dot(acc[:, P_LR:P_LR + LANES].astype(BF16), gm_ref[...], preferred_element_type=F32) + gb_ref[...]
    la_ref[...] = (jnp.minimum(z, 0.0) - jnp.log(1.0 + jnp.exp(-jnp.abs(z)))) * (1.0 / GLA_TAU)

    fx = acc[:, P_FX:P_FX + FNET_WIDTH].astype(BF16)
    zc_ref[...] = jnp.dot(fx, cc_ref[...], preferred_element_type=F32).astype(BF16)
    zs_ref[...] = jnp.dot(fx, sc_ref[...], preferred_element_type=F32).astype(BF16)


def _proj(h, w, cos, sin, qg, kg, bd, gm, gb, cc, sc, batch, seq, tm):
    n, d = h.shape
    nst = seq // tm
    row = lambda w_: pl.BlockSpec((tm, w_), lambda i: (i, 0))
    const = lambda a: pl.BlockSpec(a.shape, lambda i: (0,) * a.ndim)
    tab = pl.BlockSpec((tm, LANES), lambda i: (i % nst, 0))
    zspec = pl.BlockSpec((tm, FNET_WIDTH), lambda i: (i % nst, i // nst))
    outs = [(ATT_WIDTH, BF16), (2 * ATT_KV_WIDTH, BF16), (ATT_KV_WIDTH, BF16), (GLA_QK_WIDTH, BF16),
            (GLA_QK_WIDTH, BF16), (GLA_WIDTH, BF16), (GLA_WIDTH, BF16), (2 * GLA_QK_WIDTH, F32)]
    return pl.pallas_call(
        _proj_kernel,
        grid=(n // tm,),
        in_specs=[row(d), const(w), tab, tab, const(qg), const(kg), const(bd), const(gm),
                  const(gb), const(cc), const(sc)],
        out_specs=[row(w_) for w_, _ in outs] + [zspec, zspec],
        out_shape=[jax.ShapeDtypeStruct((n, w_), dt) for w_, dt in outs]
        + [jax.ShapeDtypeStruct((seq, batch * FNET_WIDTH), BF16)] * 2,
        compiler_params=_cparams("parallel"),
        name="in_proj",
    )(h, w, cos, sin, qg, kg, bd, gm, gb, cc, sc)


ATT_KEY_CHUNK = 512
ATT_V_ROWS = 128


def _attn_kernel(q_ref, k_ref, vt_ref, o_ref):
    s_len = k_ref.shape[1]
    kc = min(ATT_KEY_CHUNK, s_len)
    nkc = s_len // kc
    lane = lax.broadcasted_iota(jnp.int32, (1, LANES), 1)
    low = lane < HEAD_DIM
    heads = [(p, hh) for p in range(2) for hh in range(2)]
    qms = []
    for p, hh in heads:
        qp = q_ref[0, :, p * LANES:(p + 1) * LANES]
        qms.append(jnp.where(low if hh == 0 else jnp.logical_not(low), qp, jnp.zeros_like(qp)))

    def scores(c):
        kblk = k_ref[0, c * kc:(c + 1) * kc, :]
        return [lax.dot_general(kblk, qm, NT_DIMS, preferred_element_type=F32) for qm in qms]

    m = [None] * len(heads)
    acc = [None] * len(heads)
    s_next = scores(0)
    for c in range(nkc):
        s_cur = s_next
        if c + 1 < nkc:
            s_next = scores(c + 1)
        vt = vt_ref[0, 0, :, c * kc:(c + 1) * kc]
        for i in range(len(heads)):
            mc = jnp.max(s_cur[i], axis=0, keepdims=True)
            m_new = mc if c == 0 else jnp.maximum(m[i], mc)
            pe = jnp.exp2(s_cur[i] - m_new).astype(BF16)
            pv = jnp.dot(vt, pe, preferred_element_type=F32)
            acc[i] = pv if c == 0 else acc[i] * jnp.exp2(m[i] - m_new) + pv
            m[i] = m_new
    outs = [a[0:HEAD_DIM] / a[HEAD_DIM:HEAD_DIM + 1] for a in acc]
    o_ref[0] = jnp.transpose(jnp.concatenate(outs, axis=0)).astype(BF16)


def _attention(q, k2, vt, tq):
    b, s, _ = q.shape
    gw = ATT_WIDTH // ATT_KV_HEADS
    return pl.pallas_call(
        _attn_kernel,
        grid=(b, ATT_KV_HEADS, s // tq),
        in_specs=[pl.BlockSpec((1, tq, gw), lambda bi, g, qi: (bi, qi, g)),
                  pl.BlockSpec((1, s, LANES), lambda bi, g, qi: (bi, 0, g)),
                  pl.BlockSpec((1, 1, ATT_V_ROWS, s), lambda bi, g, qi: (bi, g, 0, 0))],
        out_specs=pl.BlockSpec((1, tq, gw), lambda bi, g, qi: (bi, qi, g)),
        out_shape=jax.ShapeDtypeStruct((b, s, ATT_WIDTH), BF16),
        compiler_params=_cparams("parallel", "parallel", "parallel"),
        name="gqa_attention",
    )(q, k2, vt)


GLA_TILE = 128


def _gla_constants(t):
    nl = int(math.log2(t))
    i = np.arange(t)[:, None]
    m = np.arange(t)[None, :]
    fwd = [m <= i, m > i]
    bwd = [m >= i, m < i]
    mask_f, mask_b = [], []
    for l in range(nl):
        s = t >> l
        half = s // 2
        blk = (i // s) * s
        ref = blk + half - 1
        right = (i - blk) >= half
        wf = np.where(right, (m > ref) & (m <= i), (m > i) & (m <= ref))
        wb = np.where(right, (m > ref) & (m < i), (m >= i) & (m <= ref))
        fwd.append(wf)
        bwd.append(wb)
        same = (i // s) == (m // s)
        mask_f.append(same)
        mask_b.append(same)
    mask_f.append(i == m)
    wst = np.stack([np.concatenate(fwd, 0), np.concatenate(bwd, 0)]).astype(np.float32)
    tile4 = lambda a: np.tile(a.astype(np.float32), (GLA_HEADS, 1))
    mf = np.stack([tile4(a) for a in mask_f])
    mb = np.stack([tile4(a) for a in mask_b])
    return nl, wst, mf, mb


def _gla_kernel(nl, t, q_ref, k_ref, v_ref, la_ref, r_ref, w_ref, mf_ref, mb_ref, bdm_ref, bdn_ref, g_ref,
                o_ref, st_ref, acc_ref):
    s = q_ref.shape[1]
    nt = s // t
    row = lax.broadcasted_iota(jnp.int32, (t, GLA_QK_WIDTH), 0)
    lane_k = lax.broadcasted_iota(jnp.int32, (1, GLA_QK_WIDTH), 1)
    lane_v = lax.broadcasted_iota(jnp.int32, (1, GLA_WIDTH), 1)
    head_k = [(lane_k // GLA_DK) == h for h in range(GLA_HEADS)]
    head_v = [(lane_v // GLA_DV) == h for h in range(GLA_HEADS)]

    dirs = (0, 1)

    def stack_heads(a):
        return jnp.concatenate([jnp.where(head_k[h], a, 0.0) for h in range(GLA_HEADS)], axis=0).astype(BF16)

    def tile_pair(t0s):
        q, k, v, x, o_inter, sc = [], [], [], [], [], []
        for d in dirs:
            t0 = t0s[d]
            q.append(q_ref[0, pl.ds(t0, t), :].astype(F32))
            k.append(k_ref[0, pl.ds(t0, t), :].astype(F32))
            v.append(v_ref[0, pl.ds(t0, t), :])
            la = la_ref[0, pl.ds(t0, t), d * GLA_QK_WIDTH:(d + 1) * GLA_QK_WIDTH]
            la_hi = la.astype(BF16)
            la_lo = (la - la_hi.astype(F32)).astype(BF16)
            wst = w_ref[d]
            x.append(jnp.exp(jnp.dot(wst, la_hi, preferred_element_type=F32)
                             + jnp.dot(wst, la_lo, preferred_element_type=F32)))
        for d in dirs:
            xq = x[d][0:t]
            q_st = (q[d] * xq).astype(BF16)
            k_st = (k[d] * x[d][t:2 * t]).astype(BF16)
            dec = xq[t - 1:t] if d == 0 else xq[0:1]
            st = st_ref[d]
            o_inter.append(lax.dot_general(q_st, st.astype(BF16), NT_DIMS, preferred_element_type=F32))
            kv = lax.dot_general(v[d], k_st, TN_DIMS, preferred_element_type=F32)
            st_ref[d] = st * dec + kv * bdm_ref[...]
            sc.append(jnp.zeros((GLA_HEADS * t, t), F32))
        for l in range(nl):
            half = t >> (l + 1)
            right = ((row // half) % 2) == 1
            for d in dirs:
                qside = right if d == 0 else jnp.logical_not(right)
                g = jnp.where(qside, q[d], k[d]) * x[d][(2 + l) * t:(3 + l) * t]
                a4 = stack_heads(jnp.where(qside, g, 0.0))
                b = jnp.where(qside, 0.0, g).astype(BF16)
                m_ref = mf_ref if d == 0 else mb_ref
                sc[d] = sc[d] + lax.dot_general(a4, b, NT_DIMS, preferred_element_type=F32) * m_ref[l]
        sc[0] = sc[0] + lax.dot_general(stack_heads(q[0]), k[0].astype(BF16), NT_DIMS,
                                        preferred_element_type=F32) * mf_ref[nl]
        outs = []
        for d in dirs:
            o4 = jnp.dot(sc[d].astype(BF16), v[d], preferred_element_type=F32)
            o = o_inter[d]
            for h in range(GLA_HEADS):
                o = o + jnp.where(head_v[h], o4[h * t:(h + 1) * t], 0.0)
            outs.append(o)
        return outs

    def finish(t0, tot):
        ss = jnp.dot((tot * tot).astype(BF16), bdn_ref[...], preferred_element_type=F32)
        r = r_ref[0, pl.ds(t0, t), :].astype(F32)
        y = tot * lax.rsqrt(ss * (1.0 / GLA_DV) + RMS_EPS) * g_ref[...] * (r / (1.0 + jnp.exp(-r)))
        o_ref[0, pl.ds(t0, t), :] = y.astype(BF16)

    def tiles(i):
        return pl.multiple_of(i * t, t), pl.multiple_of((nt - 1 - i) * t, t)

    st_ref[...] = jnp.zeros_like(st_ref)

    def park(i, c):
        t0s = tiles(i)
        for t0, o in zip(t0s, tile_pair(t0s)):
            acc_ref[pl.ds(t0, t), :] = o
        return c

    def complete(i, c):
        t0s = tiles(i)
        for t0, o in zip(t0s, tile_pair(t0s)):
            finish(t0, acc_ref[pl.ds(t0, t), :] + o)
        return c

    lax.fori_loop(0, nt // 2, park, 0)
    lax.fori_loop(nt // 2, nt, complete, 0)


def _gla(gq, gk, gv, la, gr, wst, mf, mb, bdm, bdn, g, nl, t):
    b, s, _ = gq.shape
    seq = lambda w_: pl.BlockSpec((1, s, w_), lambda bi: (bi, 0, 0))
    const = lambda a: pl.BlockSpec(a.shape, lambda bi: (0,) * a.ndim)
    return pl.pallas_call(
        functools.partial(_gla_kernel, nl, t),
        grid=(b,),
        in_specs=[seq(GLA_QK_WIDTH), seq(GLA_QK_WIDTH), seq(GLA_WIDTH), seq(2 * GLA_QK_WIDTH), seq(GLA_WIDTH),
                  const(wst), const(mf), const(mb), const(bdm), const(bdn), const(g)],
        out_specs=seq(GLA_WIDTH),
        out_shape=jax.ShapeDtypeStruct((b, s, GLA_WIDTH), BF16),
        scratch_shapes=[pltpu.VMEM((2, GLA_WIDTH, GLA_QK_WIDTH), F32), pltpu.VMEM((s, GLA_WIDTH), F32)],
        compiler_params=_cparams("parallel"),
        name="gla_scan",
    )(gq, gk, gv, la, gr, wst, mf, mb, bdm, bdn, g)


def _dft_kernel(fc_ref, fs_ref, zc_ref, zs_ref, o_ref):
    o_ref[...] = (jnp.dot(fc_ref[...], zc_ref[...], preferred_element_type=F32)
                  + jnp.dot(fs_ref[...], zs_ref[...], preferred_element_type=F32)).astype(BF16)


def _position_dft(fc, fs, zc, zs, tm, tn):
    s, w = zc.shape
    return pl.pallas_call(
        _dft_kernel,
        grid=(w // tn, s // tm),
        in_specs=[pl.BlockSpec((tm, s), lambda j, i: (i, 0)),
                  pl.BlockSpec((tm, s), lambda j, i: (i, 0)),
                  pl.BlockSpec((s, tn), lambda j, i: (0, j)),
                  pl.BlockSpec((s, tn), lambda j, i: (0, j))],
        out_specs=pl.BlockSpec((tm, tn), lambda j, i: (i, j)),
        out_shape=jax.ShapeDtypeStruct((s, w), BF16),
        compiler_params=_cparams("parallel", "parallel"),
        name="position_dft",
    )(fc, fs, zc, zs)


def _outproj_kernel(alpha, n_exp, oatt_ref, ogla_ref, offt_ref, h_ref, wo_ref, g_ref, b_ref,
                    wrh_ref, wrl_ref, rb_ref, us_ref,
                    h1_ref, idx_ref, wts_ref, rank_ref, cnt_ref, carry_ref):
    i = pl.program_id(0)

    @pl.when(i == 0)
    def _():
        carry_ref[...] = jnp.zeros_like(carry_ref)

    acc = jnp.dot(oatt_ref[...], wo_ref[0:ATT_WIDTH, :], preferred_element_type=F32)
    acc = acc + jnp.dot(ogla_ref[...], wo_ref[ATT_WIDTH:ATT_WIDTH + GLA_WIDTH, :], preferred_element_type=F32)
    acc = acc + jnp.dot(offt_ref[...], wo_ref[ATT_WIDTH + GLA_WIDTH:, :], preferred_element_type=F32)
    h1 = _layer_norm_rows(alpha * h_ref[...] + acc, g_ref[...], b_ref[...])
    h1_ref[...] = h1

    hh = h1.astype(BF16)
    hl = (h1 - hh.astype(F32)).astype(BF16)
    wrh = wrh_ref[...]
    logits = (lax.dot_general(wrh, hh, NT_DIMS, preferred_element_type=F32)
              + lax.dot_general(wrh, hl, NT_DIMS, preferred_element_type=F32)
              + lax.dot_general(wrl_ref[...], hh, NT_DIMS, preferred_element_type=F32)) + rb_ref[...]
    tm = logits.shape[1]
    e_iota = lax.broadcasted_iota(jnp.int32, (n_exp, tm), 0)
    cur = logits
    vals, idxs, sels = [], [], []
    for _k in range(TOP_K):
        m = jnp.max(cur, axis=0, keepdims=True)
        ik = jnp.min(jnp.where(cur == m, e_iota, n_exp), axis=0, keepdims=True)
        sel = e_iota == ik
        vals.append(m)
        idxs.append(ik)
        sels.append(sel)
        cur = jnp.where(sel, -jnp.inf, cur)
    ex = [jnp.exp(v - vals[0]) for v in vals]
    den = ex[0] + ex[1] + ex[2] + ex[3]
    zero = jnp.zeros_like(den)
    idx_ref[...] = jnp.concatenate(idxs, axis=0)
    wts_ref[...] = jnp.concatenate([e / den for e in ex] + [zero] * (wts_ref.shape[0] - TOP_K), axis=0)

    onehot = jnp.zeros((n_exp, tm), F32)
    for sel in sels:
        onehot = onehot + sel.astype(F32)
    before = jnp.dot(onehot.astype(BF16), us_ref[...], preferred_element_type=F32) + carry_ref[:, 0:1]
    ranks = [jnp.sum(jnp.where(sel, before, 0.0), axis=0, keepdims=True) for sel in sels]
    rank_ref[...] = jnp.concatenate(ranks, axis=0).astype(jnp.int32)
    carry = carry_ref[...] + jnp.sum(onehot, axis=1, keepdims=True)
    carry_ref[...] = carry
    cnt_ref[...] = carry.astype(jnp.int32)


def _outproj(alpha, o_att, o_gla, o_fft, h, wo, g, b, wrh, wrl, rb, us, seq, tm):
    n, d = h.shape
    n_exp = wrh.shape[0]
    nst = seq // tm
    row = lambda w_: pl.BlockSpec((tm, w_), lambda i: (i, 0))
    const = lambda a: pl.BlockSpec(a.shape, lambda i: (0,) * a.ndim)
    tok = lambda r: pl.BlockSpec((r, tm), lambda i: (0, i))
    return pl.pallas_call(
        functools.partial(_outproj_kernel, alpha, n_exp),
        grid=(n // tm,),
        in_specs=[row(ATT_WIDTH), row(GLA_WIDTH),
                  pl.BlockSpec((tm, FNET_WIDTH), lambda i: (i % nst, i // nst)),
                  row(d), const(wo), const(g), const(b), const(wrh), const(wrl), const(rb), const(us)],
        out_specs=[row(d), tok(TOP_K), tok(2 * TOP_K), tok(TOP_K),
                   pl.BlockSpec((n_exp, LANES), lambda i: (0, 0))],
        out_shape=[jax.ShapeDtypeStruct((n, d), F32),
                   jax.ShapeDtypeStruct((TOP_K, n), jnp.int32),
                   jax.ShapeDtypeStruct((2 * TOP_K, n), F32),
                   jax.ShapeDtypeStruct((TOP_K, n), jnp.int32),
                   jax.ShapeDtypeStruct((n_exp, LANES), jnp.int32)],
        scratch_shapes=[pltpu.VMEM((n_exp, LANES), F32)],
        compiler_params=_cparams("arbitrary"),
        name="out_proj_router",
    )(o_att, o_gla, o_fft, h, wo, g, b, wrh, wrl, rb, us)


DMA_UNROLL = 8
ROW_SUB = 8


def _row_copy(src, dst, sem):
    return pltpu.make_async_copy(src, dst, sem)


def _dispatch_kernel(tb, nb, dest_ref, h_ref, xs_in_ref, xs_ref, buf, sem):
    del xs_in_ref
    i = pl.program_id(0)
    slot = i % 2
    buf[slot] = h_ref[...].reshape(buf.shape[1:])

    def start(t, c):
        for k in range(TOP_K):
            _row_copy(buf.at[slot, t], xs_ref.at[dest_ref[k, t]], sem.at[slot]).start(priority=k % 2)
        return c

    lax.fori_loop(0, tb, start, 0, unroll=DMA_UNROLL)

    def wait_block(s):
        def wait(t, c):
            for k in range(TOP_K):
                _row_copy(buf.at[s, 0], xs_ref.at[0], sem.at[s]).wait()
            return c
        lax.fori_loop(0, tb, wait, 0, unroll=DMA_UNROLL)

    @pl.when(i > 0)
    def _():
        wait_block(1 - slot)

    @pl.when(i == nb - 1)
    def _():
        wait_block(slot)


def _dispatch(dest, h, xs_prev, tb):
    n, d = h.shape
    return pl.pallas_call(
        functools.partial(_dispatch_kernel, tb, n // tb),
        grid=(n // tb,),
        in_specs=[pl.BlockSpec((TOP_K, tb), lambda i: (0, i), memory_space=pltpu.SMEM),
                  pl.BlockSpec((tb, d), lambda i: (i, 0)),
                  pl.BlockSpec(memory_space=pl.ANY)],
        out_specs=pl.BlockSpec(memory_space=pl.ANY),
        out_shape=jax.ShapeDtypeStruct(xs_prev.shape, F32),
        scratch_shapes=[pltpu.VMEM((2, tb, ROW_SUB, d // ROW_SUB), F32), pltpu.SemaphoreType.DMA((2,))],
        input_output_aliases={2: 0},
        compiler_params=_cparams("arbitrary"),
        name="moe_dispatch",
    )(dest, h, xs_prev)


def _ffn_kernel(te_ref, tv_ref, x_ref, wgu_ref, bgu_ref, wd_ref, bd_ref, y_ref, wgu_bf, wd_bf):
    j = pl.program_id(0)
    valid = tv_ref[j]
    new_expert = jnp.logical_or(j == 0, te_ref[j] != te_ref[jnp.maximum(j - 1, 0)])

    @pl.when(jnp.logical_and(new_expert, valid > 0))
    def _():
        wgu_bf[...] = wgu_ref[0, 0].astype(BF16)
        wd_bf[...] = wd_ref[0, 0].astype(BF16)

    @pl.when(valid > 0)
    def _():
        de, d = wd_bf.shape
        x = x_ref[...].reshape(x_ref.shape[0], d).astype(BF16)
        gu = jnp.dot(x, wgu_bf[...], preferred_element_type=F32) + bgu_ref[0, 0]
        gate = jnp.minimum(gu[:, :de], SWIGLU_LIMIT)
        up = jnp.clip(gu[:, de:], -SWIGLU_LIMIT, SWIGLU_LIMIT)
        hid = (up + 1.0) * (gate / (1.0 + jnp.exp(-SWIGLU_ALPHA * gate)))
        y = jnp.dot(hid.astype(BF16), wd_bf[...], preferred_element_type=F32) + bd_ref[0, 0]
        y_ref[...] = y.reshape(y_ref.shape)

    @pl.when(valid == 0)
    def _():
        y_ref[...] = jnp.zeros_like(y_ref)


def _expert_ffn(layer, tile_expert, tile_valid, xs, wgu, bgu, wd, bd, tmf):
    rows = xs.shape[0]
    depth, n_exp, d, de2 = wgu.shape
    de = de2 // 2
    xspec = pl.BlockSpec((tmf,) + xs.shape[1:], lambda j, te, tv: (j, 0, 0))
    wspec = lambda r, c: pl.BlockSpec((1, 1, r, c), lambda j, te, tv: (layer, te[j], 0, 0))
    grid_spec = pltpu.PrefetchScalarGridSpec(
        num_scalar_prefetch=2,
        grid=(rows // tmf,),
        in_specs=[xspec, wspec(d, de2), wspec(1, de2), wspec(de, d), wspec(1, d)],
        out_specs=xspec,
        scratch_shapes=[pltpu.VMEM((d, de2), BF16), pltpu.VMEM((de, d), BF16)],
    )
    return pl.pallas_call(
        _ffn_kernel,
        grid_spec=grid_spec,
        out_shape=jax.ShapeDtypeStruct(xs.shape, F32),
        compiler_params=_cparams("arbitrary"),
        name="moe_expert_ffn",
    )(tile_expert, tile_valid, xs, wgu, bgu.reshape(depth, n_exp, 1, de2), wd, bd.reshape(depth, n_exp, 1, d))


def _combine_kernel(alpha, tb, nb, dcur_ref, dnxt_ref, w_ref, h1_ref, g_ref, b_ref, y_ref, o_ref, ybuf, sem):
    i = pl.program_id(0)
    slot = i % 2
    nslot = 1 - slot
    d = h1_ref.shape[1]

    def start_block(dref, s):
        def start(t, c):
            for k in range(TOP_K):
                _row_copy(y_ref.at[dref[k, t]], ybuf.at[s, k, t], sem.at[s]).start(priority=k % 2)
            return c
        lax.fori_loop(0, tb, start, 0, unroll=DMA_UNROLL)

    def wait_block(s):
        def wait(t, c):
            for k in range(TOP_K):
                _row_copy(y_ref.at[0], ybuf.at[s, k, 0], sem.at[s]).wait()
            return c
        lax.fori_loop(0, tb, wait, 0, unroll=DMA_UNROLL)

    @pl.when(i == 0)
    def _():
        start_block(dcur_ref, slot)

    @pl.when(i + 1 < nb)
    def _():
        start_block(dnxt_ref, nslot)

    wait_block(slot)
    wt = jnp.transpose(w_ref[...])
    acc = alpha * h1_ref[...]
    for k in range(TOP_K):
        acc = acc + ybuf[slot, k].reshape(tb, d) * wt[:, k:k + 1]
    o_ref[...] = _layer_norm_rows(acc, g_ref[...], b_ref[...])


def _combine(alpha, dest, wts, h1, g, b, ys, tb):
    n, d = h1.shape
    nb = n // tb
    dspec = lambda f: pl.BlockSpec((TOP_K, tb), f, memory_space=pltpu.SMEM)
    return pl.pallas_call(
        functools.partial(_combine_kernel, alpha, tb, nb),
        grid=(nb,),
        in_specs=[dspec(lambda i: (0, i)),
                  dspec(lambda i: (0, jnp.minimum(i + 1, nb - 1))),
                  pl.BlockSpec((2 * TOP_K, tb), lambda i: (0, i)),
                  pl.BlockSpec((tb, d), lambda i: (i, 0)),
                  pl.BlockSpec((1, d), lambda i: (0, 0)),
                  pl.BlockSpec((1, d), lambda i: (0, 0)),
                  pl.BlockSpec(memory_space=pl.ANY)],
        out_specs=pl.BlockSpec((tb, d), lambda i: (i, 0)),
        out_shape=jax.ShapeDtypeStruct((n, d), F32),
        scratch_shapes=[pltpu.VMEM((2, TOP_K, tb) + ys.shape[1:], F32),
                        pltpu.SemaphoreType.DMA((2,))],
        compiler_params=_cparams("arbitrary"),
        name="moe_combine",
    )(dest, dest, wts, h1, g.reshape(1, d), b.reshape(1, d), ys)


def _rope_tables(seq):
    pos = jnp.arange(seq)
    row_id = (pos // GRID_W).astype(F32)
    col_id = (pos % GRID_W).astype(F32)
    inv_freq = 1.0 / (ROPE_THETA ** (jnp.arange(ROPE_FREQS, dtype=F32) / ROPE_FREQS))
    lane = np.arange(LANES)
    hd = lane % HEAD_DIM
    freq = hd % ROPE_FREQS
    use_col = (hd // (2 * ROPE_FREQS)) == 1
    second = (hd % (2 * ROPE_FREQS)) >= ROPE_FREQS
    ang = jnp.where(use_col[None, :], col_id[:, None], row_id[:, None]) * inv_freq[freq][None, :]
    sign = np.where(second, 1.0, -1.0).astype(np.float32)
    return jnp.cos(ang), jnp.sin(ang) * sign[None, :]


def _dft_tables(seq):
    a = _pick(FNET_GROUP_DIM, seq)
    t = jnp.arange(seq, dtype=jnp.int32)

    def cos_sin(rows):
        ang = ((rows[:, None] * t[None, :]) % seq).astype(F32) * (2.0 * math.pi / seq)
        return jnp.cos(ang), jnp.sin(ang)

    cq, sq = [v[:, None, :] for v in cos_sin(jnp.arange(seq // a, dtype=jnp.int32) * a)]
    cr, sr = [v[None, :, :] for v in cos_sin(jnp.arange(a, dtype=jnp.int32))]
    scale = seq ** -0.5
    fc = ((cq * cr - sq * sr) * scale).reshape(seq, seq).astype(BF16)
    fs = ((sq * cr + cq * sr) * -scale).reshape(seq, seq).astype(BF16)
    c = np.arange(FNET_GROUP_DIM)
    angc = 2.0 * np.pi * ((c[:, None] * c[None, :]) % FNET_GROUP_DIM) / FNET_GROUP_DIM
    eye = np.eye(FNET_GROUPS)
    cc = np.kron(eye, np.cos(angc)) * FNET_GROUP_DIM ** -0.5
    sc = np.kron(eye, np.sin(angc)) * FNET_GROUP_DIM ** -0.5
    return fc, fs, jnp.asarray(cc, BF16), jnp.asarray(sc, BF16)


def _block_ones(width, group):
    g = np.arange(width) // group
    return jnp.asarray((g[:, None] == g[None, :]).astype(np.float32), BF16)


def _pick(limit, n):
    t = min(limit, n)
    while n % t:
        t //= 2
    return t


def kernel(x, ln_in_g, ln_in_b, w_in, att_q_gain, att_k_gain, gla_gate_w, gla_gate_b, gla_norm_g, w_out,
           ln1_g, ln1_b, router_w, router_b, exp_w_gu, exp_b_gu, exp_w_down, exp_b_down, ln2_g, ln2_b):
    batch, seq, d = x.shape
    n = batch * seq
    depth = w_in.shape[0]
    n_exp = router_w.shape[-1]
    alpha = (2.0 * depth) ** 0.25

    tm = _pick(512, seq)
    tmi = _pick(1024, seq)
    tq = _pick(512, seq)
    tg = _pick(GLA_TILE, seq // 2)
    tmf = _pick(1024, n)
    tb = _pick(512, n)
    n_tiles = (TOP_K * n) // tmf + n_exp
    rows = n_tiles * tmf

    cos, sin = _rope_tables(seq)
    fc, fs, cc, sc = _dft_tables(seq)
    bd_head = _block_ones(LANES, HEAD_DIM)
    bd_gla = _block_ones(GLA_WIDTH, GLA_DV)
    vt_pad = jnp.zeros((batch, ATT_KV_HEADS, ATT_V_ROWS - HEAD_DIM, seq), BF16).at[:, :, 0, :].set(1.0)
    nl, wst, mf, mb = _gla_constants(tg)
    wst = jnp.asarray(wst, BF16)
    mf = jnp.asarray(mf)
    mb = jnp.asarray(mb)
    hv = np.arange(GLA_WIDTH) // GLA_DV
    hk = np.arange(GLA_QK_WIDTH) // GLA_DK
    bdm = jnp.asarray((hv[:, None] == hk[None, :]).astype(np.float32))
    us = np.arange(tm)
    us = jnp.asarray((us[:, None] < us[None, :]).astype(np.float32), BF16)

    w_pad = _pad_in_proj(w_in)
    qg = jnp.tile(att_q_gain, (1, LANES // HEAD_DIM)).reshape(depth, 1, LANES)
    kg = jnp.tile(att_k_gain, (1, LANES // HEAD_DIM)).reshape(depth, 1, LANES)
    gm = jnp.zeros((depth, LANES, 2 * GLA_QK_WIDTH), F32)
    gm = gm.at[:, :GLA_GATE_RANK, :GLA_QK_WIDTH].set(gla_gate_w[:, 0])
    gm = gm.at[:, GLA_GATE_RANK:2 * GLA_GATE_RANK, GLA_QK_WIDTH:].set(gla_gate_w[:, 1]).astype(BF16)
    gb = gla_gate_b.reshape(depth, 1, 2 * GLA_QK_WIDTH)
    gn = jnp.tile(gla_norm_g, (1, GLA_HEADS)).reshape(depth, 1, GLA_WIDTH)
    wo = w_out.astype(BF16)
    wr_t = jnp.swapaxes(router_w, 1, 2)
    wrh = wr_t.astype(BF16)
    wrl = (wr_t - wrh.astype(F32)).astype(BF16)
    rb = router_b.reshape(depth, n_exp, 1)

    e_ids = jnp.arange(n_exp, dtype=jnp.int32)
    tile_ids = jnp.arange(n_tiles, dtype=jnp.int32)
    xs = jnp.zeros((rows, ROW_SUB, d // ROW_SUB), F32)

    h = _layer_norm(x.reshape(n, d), ln_in_g, ln_in_b, tm)
    for l in range(depth):
        q, k2, av, gq, gk, gv, gr, la, zc, zs = _proj(
            h, w_pad[l], cos, sin, qg[l], kg[l], bd_head, gm[l], gb[l], cc, sc, batch, seq, tmi)
        vt = jnp.transpose(av.reshape(batch, seq, ATT_KV_HEADS, HEAD_DIM), (0, 2, 3, 1))
        vt = jnp.concatenate([vt, vt_pad], axis=2)
        o_att = _attention(q.reshape(batch, seq, -1), k2.reshape(batch, seq, -1), vt, tq)
        o_gla = _gla(gq.reshape(batch, seq, -1), gk.reshape(batch, seq, -1), gv.reshape(batch, seq, -1),
                     la.reshape(batch, seq, -1), gr.reshape(batch, seq, -1), wst, mf, mb, bdm, bd_gla, gn[l], nl, tg)
        o_fft = _position_dft(fc, fs, zc, zs, tm, _pick(512, batch * FNET_WIDTH))
        h1, idx, wts, rank, cnt = _outproj(
            alpha, o_att.reshape(n, -1), o_gla.reshape(n, -1), o_fft, h, wo[l], ln1_g[l].reshape(1, d),
            ln1_b[l].reshape(1, d), wrh[l], wrl[l], rb[l], us, seq, tm)

        counts = cnt[:, 0]
        tiles_per = (counts + tmf - 1) // tmf
        tile_end = jnp.cumsum(tiles_per)
        offs = (tile_end - tiles_per) * tmf
        chosen = idx[:, :, None] == e_ids[None, None, :]
        dest = rank + jnp.sum(jnp.where(chosen, offs[None, None, :], 0), axis=-1)
        used = tile_end[-1]
        tile_of = jnp.minimum(tile_ids, used - 1)
        tile_expert = jnp.sum((tile_end[None, :] <= tile_of[:, None]).astype(jnp.int32), axis=1)
        own = tile_expert[:, None] == e_ids[None, :]
        group_end = jnp.sum(jnp.where(own, (offs + counts)[None, :], 0), axis=1)
        tile_valid = jnp.where(tile_ids < used, jnp.clip(group_end - tile_ids * tmf, 0, tmf), 0).astype(jnp.int32)

        xs = _dispatch(dest, h1, xs, tb)
        ys = _expert_ffn(l, tile_expert, tile_valid, xs, exp_w_gu, exp_b_gu, exp_w_down, exp_b_down, tmf)
        h = _combine(alpha, dest, wts, h1, ln2_g[l], ln2_b[l], ys, tb)
    return h.reshape(batch, seq, d)
```

```python
import functools
import math

import numpy as np
import jax
import jax.numpy as jnp
from jax import lax
from jax.experimental import pallas as pl
from jax.experimental.pallas import tpu as pltpu

F32 = jnp.float32
BF16 = jnp.bfloat16

GRID_W = 64
HEAD_DIM = 64
ATT_HEADS = 8
ATT_KV_HEADS = 2
ATT_WIDTH = ATT_HEADS * HEAD_DIM
ATT_KV_WIDTH = ATT_KV_HEADS * HEAD_DIM
ROPE_THETA = 10000.0
ROPE_FREQS = HEAD_DIM // 4
GLA_HEADS = 4
GLA_DK = 32
GLA_DV = 64
GLA_QK_WIDTH = GLA_HEADS * GLA_DK
GLA_WIDTH = GLA_HEADS * GLA_DV
GLA_GATE_RANK = 16
GLA_TAU = 16.0
FNET_GROUPS = 4
FNET_GROUP_DIM = 64
FNET_WIDTH = FNET_GROUPS * FNET_GROUP_DIM
TOP_K = 4
SWIGLU_LIMIT = 7.0
SWIGLU_ALPHA = 1.702
LN_EPS = 1e-5
RMS_EPS = 1e-6

LANES = 128
VMEM_LIMIT_BYTES = 56 * 1024 * 1024

NT_DIMS = (((1,), (1,)), ((), ()))
TN_DIMS = (((0,), (0,)), ((), ()))


def _cparams(*sem):
    return pltpu.CompilerParams(dimension_semantics=sem, vmem_limit_bytes=VMEM_LIMIT_BYTES)


def _layer_norm_rows(x, g, b):
    mu = jnp.mean(x, axis=-1, keepdims=True)
    xc = x - mu
    var = jnp.mean(xc * xc, axis=-1, keepdims=True)
    return xc * lax.rsqrt(var + LN_EPS) * g + b


def _ln_kernel(x_ref, g_ref, b_ref, o_ref):
    o_ref[...] = _layer_norm_rows(x_ref[...], g_ref[...], b_ref[...])


def _layer_norm(x, g, b, tm):
    n, d = x.shape
    return pl.pallas_call(
        _ln_kernel,
        grid=(n // tm,),
        in_specs=[pl.BlockSpec((tm, d), lambda i: (i, 0)),
                  pl.BlockSpec((1, d), lambda i: (0, 0)),
                  pl.BlockSpec((1, d), lambda i: (0, 0))],
        out_specs=pl.BlockSpec((tm, d), lambda i: (i, 0)),
        out_shape=jax.ShapeDtypeStruct((n, d), F32),
        compiler_params=_cparams("parallel"),
        name="ln_in",
    )(x, g.reshape(1, d), b.reshape(1, d))


P_Q = 0
P_K = P_Q + ATT_WIDTH
P_V = P_K + 2 * ATT_KV_WIDTH
P_GQ = P_V + ATT_KV_WIDTH
P_GK = P_GQ + GLA_QK_WIDTH
P_GV = P_GK + GLA_QK_WIDTH
P_GR = P_GV + GLA_WIDTH
P_LR = P_GR + GLA_WIDTH
P_FX = P_LR + LANES
P_WIDTH = P_FX + FNET_WIDTH

Q_SCALE = HEAD_DIM ** -0.5 * math.log2(math.e)


def _pad_in_proj(w_in):
    l, d, _ = w_in.shape
    o = 0
    aq = w_in[..., o:o + ATT_WIDTH]; o += ATT_WIDTH
    ak = w_in[..., o:o + ATT_KV_WIDTH]; o += ATT_KV_WIDTH
    av = w_in[..., o:o + ATT_KV_WIDTH]; o += ATT_KV_WIDTH
    gq = w_in[..., o:o + GLA_QK_WIDTH]; o += GLA_QK_WIDTH
    gk = w_in[..., o:o + GLA_QK_WIDTH]; o += GLA_QK_WIDTH
    gv = w_in[..., o:o + GLA_WIDTH]; o += GLA_WIDTH
    gr = w_in[..., o:o + GLA_WIDTH]; o += GLA_WIDTH
    lr = w_in[..., o:o + 2 * GLA_GATE_RANK]; o += 2 * GLA_GATE_RANK
    fx = w_in[..., o:o + FNET_WIDTH]
    k0, k1 = ak[..., :HEAD_DIM], ak[..., HEAD_DIM:]
    zlr = jnp.zeros((l, d, LANES - 2 * GLA_GATE_RANK), w_in.dtype)
    cols = [aq, k0, k0, k1, k1, av, gq, gk, gv, gr, lr, zlr, fx]
    return jnp.concatenate(cols, axis=-1).astype(BF16)


def _proj_kernel(h_ref, w_ref, cos_ref, sin_ref, qg_ref, kg_ref, bd_ref, gm_ref, gb_ref,
                 cc_ref, sc_ref,
                 q_ref, k2_ref, v_ref, gq_ref, gk_ref, gv_ref, gr_ref, la_ref, zc_ref, zs_ref):
    acc = jnp.dot(h_ref[...].astype(BF16), w_ref[...], preferred_element_type=F32)
    cos = cos_ref[...]
    sin = sin_ref[...]
    bd = bd_ref[...]
    lane = lax.broadcasted_iota(jnp.int32, (1, LANES), 1)
    first_half = (lane % (2 * ROPE_FREQS)) < ROPE_FREQS

    def norm_rope(x, gain, scale):
        ss = jnp.dot((x * x).astype(BF16), bd, preferred_element_type=F32)
        xn = x * lax.rsqrt(ss * (1.0 / HEAD_DIM) + RMS_EPS) * gain
        partner = jnp.where(first_half, pltpu.roll(xn, LANES - ROPE_FREQS, 1), pltpu.roll(xn, ROPE_FREQS, 1))
        return (xn * cos + partner * sin) * scale

    for c in range(ATT_WIDTH // LANES):
        x = acc[:, P_Q + c * LANES:P_Q + (c + 1) * LANES]
        q_ref[:, c * LANES:(c + 1) * LANES] = norm_rope(x, qg_ref[...], Q_SCALE).astype(BF16)
    for c in range(2 * ATT_KV_WIDTH // LANES):
        x = acc[:, P_K + c * LANES:P_K + (c + 1) * LANES]
        k2_ref[:, c * LANES:(c + 1) * LANES] = norm_rope(x, kg_ref[...], 1.0).astype(BF16)
    v_ref[...] = acc[:, P_V:P_V + ATT_KV_WIDTH].astype(BF16)

    gq_ref[...] = (acc[:, P_GQ:P_GQ + GLA_QK_WIDTH] * (GLA_DK ** -0.5)).astype(BF16)
    gk_ref[...] = acc[:, P_GK:P_GK + GLA_QK_WIDTH].astype(BF16)
    gv_ref[...] = acc[:, P_GV:P_GV + GLA_WIDTH].astype(BF16)
    gr_ref[...] = acc[:, P_GR:P_GR + GLA_WIDTH].astype(BF16)
    z = jnp.dot(acc[:, P_LR:P_LR + LANES].astype(BF16), gm_ref[...], preferred_element_type=F32) + gb_ref[...]
    la_ref[...] = (jnp.minimum(z, 0.0) - jnp.log(1.0 + jnp.exp(-jnp.abs(z)))) * (1.0 / GLA_TAU)

    fx = acc[:, P_FX:P_FX + FNET_WIDTH].astype(BF16)
    zc_ref[...] = jnp.dot(fx, cc_ref[...], preferred_element_type=F32).astype(BF16)
    zs_ref[...] = jnp.dot(fx, sc_ref[...], preferred_element_type=F32).astype(BF16)


def _proj(h, w, cos, sin, qg, kg, bd, gm, gb, cc, sc, batch, seq, tm):
    n, d = h.shape
    nst = seq // tm
    row = lambda w_: pl.BlockSpec((tm, w_), lambda i: (i, 0))
    const = lambda a: pl.BlockSpec(a.shape, lambda i: (0,) * a.ndim)
    tab = pl.BlockSpec((tm, LANES), lambda i: (i % nst, 0))
    zspec = pl.BlockSpec((tm, FNET_WIDTH), lambda i: (i % nst, i // nst))
    outs = [(ATT_WIDTH, BF16), (2 * ATT_KV_WIDTH, BF16), (ATT_KV_WIDTH, BF16), (GLA_QK_WIDTH, BF16),
            (GLA_QK_WIDTH, BF16), (GLA_WIDTH, BF16), (GLA_WIDTH, BF16), (2 * GLA_QK_WIDTH, F32)]
    return pl.pallas_call(
        _proj_kernel,
        grid=(n // tm,),
        in_specs=[row(d), const(w), tab, tab, const(qg), const(kg), const(bd), const(gm),
                  const(gb), const(cc), const(sc)],
        out_specs=[row(w_) for w_, _ in outs] + [zspec, zspec],
        out_shape=[jax.ShapeDtypeStruct((n, w_), dt) for w_, dt in outs]
        + [jax.ShapeDtypeStruct((seq, batch * FNET_WIDTH), BF16)] * 2,
        compiler_params=_cparams("parallel"),
        name="in_proj",
    )(h, w, cos, sin, qg, kg, bd, gm, gb, cc, sc)


ATT_KEY_CHUNK = 512
ATT_V_ROWS = 128


def _attn_kernel(q_ref, k_ref, vt_ref, o_ref):
    s_len = k_ref.shape[1]
    kc = min(ATT_KEY_CHUNK, s_len)
    nkc = s_len // kc
    lane = lax.broadcasted_iota(jnp.int32, (1, LANES), 1)
    low = lane < HEAD_DIM
    heads = [(p, hh) for p in range(2) for hh in range(2)]
    qms = []
    for p, hh in heads:
        qp = q_ref[0, :, p * LANES:(p + 1) * LANES]
        qms.append(jnp.where(low if hh == 0 else jnp.logical_not(low), qp, jnp.zeros_like(qp)))

    def scores(c):
        kblk = k_ref[0, c * kc:(c + 1) * kc, :]
        return [lax.dot_general(kblk, qm, NT_DIMS, preferred_element_type=F32) for qm in qms]

    m = [None] * len(heads)
    acc = [None] * len(heads)
    s_next = scores(0)
    for c in range(nkc):
        s_cur = s_next
        if c + 1 < nkc:
            s_next = scores(c + 1)
        vt = vt_ref[0, 0, :, c * kc:(c + 1) * kc]
        for i in range(len(heads)):
            mc = jnp.max(s_cur[i], axis=0, keepdims=True)
            m_new = mc if c == 0 else jnp.maximum(m[i], mc)
            pe = jnp.exp2(s_cur[i] - m_new).astype(BF16)
            pv = jnp.dot(vt, pe, preferred_element_type=F32)
            acc[i] = pv if c == 0 else acc[i] * jnp.exp2(m[i] - m_new) + pv
            m[i] = m_new
    outs = [a[0:HEAD_DIM] / a[HEAD_DIM:HEAD_DIM + 1] for a in acc]
    o_ref[0] = jnp.transpose(jnp.concatenate(outs, axis=0)).astype(BF16)


def _attention(q, k2, vt, tq):
    b, s, _ = q.shape
    gw = ATT_WIDTH // ATT_KV_HEADS
    return pl.pallas_call(
        _attn_kernel,
        grid=(b, ATT_KV_HEADS, s // tq),
        in_specs=[pl.BlockSpec((1, tq, gw), lambda bi, g, qi: (bi, qi, g)),
                  pl.BlockSpec((1, s, LANES), lambda bi, g, qi: (bi, 0, g)),
                  pl.BlockSpec((1, 1, ATT_V_ROWS, s), lambda bi, g, qi: (bi, g, 0, 0))],
        out_specs=pl.BlockSpec((1, tq, gw), lambda bi, g, qi: (bi, qi, g)),
        out_shape=jax.ShapeDtypeStruct((b, s, ATT_WIDTH), BF16),
        compiler_params=_cparams("parallel", "parallel", "parallel"),
        name="gqa_attention",
    )(q, k2, vt)


GLA_TILE = 128


def _gla_constants(t):
    nl = int(math.log2(t))
    i = np.arange(t)[:, None]
    m = np.arange(t)[None, :]
    fwd = [m <= i, m > i]
    bwd = [m >= i, m < i]
    mask_f, mask_b = [], []
    for l in range(nl):
        s = t >> l
        half = s // 2
        blk = (i // s) * s
        ref = blk + half - 1
        right = (i - blk) >= half
        wf = np.where(right, (m > ref) & (m <= i), (m > i) & (m <= ref))
        wb = np.where(right, (m > ref) & (m < i), (m >= i) & (m <= ref))
        fwd.append(wf)
        bwd.append(wb)
        same = (i // s) == (m // s)
        mask_f.append(same)
        mask_b.append(same)
    mask_f.append(i == m)
    wst = np.stack([np.concatenate(fwd, 0), np.concatenate(bwd, 0)]).astype(np.float32)
    tile4 = lambda a: np.tile(a.astype(np.float32), (GLA_HEADS, 1))
    mf = np.stack([tile4(a) for a in mask_f])
    mb = np.stack([tile4(a) for a in mask_b])
    return nl, wst, mf, mb


def _gla_kernel(nl, t, q_ref, k_ref, v_ref, la_ref, r_ref, w_ref, mf_ref, mb_ref, bdm_ref, bdn_ref, g_ref,
                o_ref, st_ref, acc_ref):
    s = q_ref.shape[1]
    nt = s // t
    row = lax.broadcasted_iota(jnp.int32, (t, GLA_QK_WIDTH), 0)
    lane_k = lax.broadcasted_iota(jnp.int32, (1, GLA_QK_WIDTH), 1)
    lane_v = lax.broadcasted_iota(jnp.int32, (1, GLA_WIDTH), 1)
    head_k = [(lane_k // GLA_DK) == h for h in range(GLA_HEADS)]
    head_v = [(lane_v // GLA_DV) == h for h in range(GLA_HEADS)]

    dirs = (0, 1)

    def stack_heads(a):
        return jnp.concatenate([jnp.where(head_k[h], a, 0.0) for h in range(GLA_HEADS)], axis=0).astype(BF16)

    def tile_pair(t0s):
        q, k, v, x, o_inter, sc = [], [], [], [], [], []
        for d in dirs:
            t0 = t0s[d]
            q.append(q_ref[0, pl.ds(t0, t), :].astype(F32))
            k.append(k_ref[0, pl.ds(t0, t), :].astype(F32))
            v.append(v_ref[0, pl.ds(t0, t), :])
            la = la_ref[0, pl.ds(t0, t), d * GLA_QK_WIDTH:(d + 1) * GLA_QK_WIDTH]
            la_hi = la.astype(BF16)
            la_lo = (la - la_hi.astype(F32)).astype(BF16)
            wst = w_ref[d]
            x.append(jnp.exp(jnp.dot(wst, la_hi, preferred_element_type=F32)
                             + jnp.dot(wst, la_lo, preferred_element_type=F32)))
        for d in dirs:
            xq = x[d][0:t]
            q_st = (q[d] * xq).astype(BF16)
            k_st = (k[d] * x[d][t:2 * t]).astype(BF16)
            dec = xq[t - 1:t] if d == 0 else xq[0:1]
            st = st_ref[d]
            o_inter.append(lax.dot_general(q_st, st.astype(BF16), NT_DIMS, preferred_element_type=F32))
            kv = lax.dot_general(v[d], k_st, TN_DIMS, preferred_element_type=F32)
            st_ref[d] = st * dec + kv * bdm_ref[...]
            sc.append(jnp.zeros((GLA_HEADS * t, t), F32))
        for l in range(nl):
            half = t >> (l + 1)
            right = ((row // half) % 2) == 1
            for d in dirs:
                qside = right if d == 0 else jnp.logical_not(right)
                g = jnp.where(qside, q[d], k[d]) * x[d][(2 + l) * t:(3 + l) * t]
                a4 = stack_heads(jnp.where(qside, g, 0.0))
                b = jnp.where(qside, 0.0, g).astype(BF16)
                m_ref = mf_ref if d == 0 else mb_ref
                sc[d] = sc[d] + lax.dot_general(a4, b, NT_DIMS, preferred_element_type=F32) * m_ref[l]
        sc[0] = sc[0] + lax.dot_general(stack_heads(q[0]), k[0].astype(BF16), NT_DIMS,
                                        preferred_element_type=F32) * mf_ref[nl]
        outs = []
        for d in dirs:
            o4 = jnp.dot(sc[d].astype(BF16), v[d], preferred_element_type=F32)
            o = o_inter[d]
            for h in range(GLA_HEADS):
                o = o + jnp.where(head_v[h], o4[h * t:(h + 1) * t], 0.0)
            outs.append(o)
        return outs

    def finish(t0, tot):
        ss = jnp.dot((tot * tot).astype(BF16), bdn_ref[...], preferred_element_type=F32)
        r = r_ref[0, pl.ds(t0, t), :].astype(F32)
        y = tot * lax.rsqrt(ss * (1.0 / GLA_DV) + RMS_EPS) * g_ref[...] * (r / (1.0 + jnp.exp(-r)))
        o_ref[0, pl.ds(t0, t), :] = y.astype(BF16)

    def tiles(i):
        return pl.multiple_of(i * t, t), pl.multiple_of((nt - 1 - i) * t, t)

    st_ref[...] = jnp.zeros_like(st_ref)

    def park(i, c):
        t0s = tiles(i)
        for t0, o in zip(t0s, tile_pair(t0s)):
            acc_ref[pl.ds(t0, t), :] = o
        return c

    def complete(i, c):
        t0s = tiles(i)
        for t0, o in zip(t0s, tile_pair(t0s)):
            finish(t0, acc_ref[pl.ds(t0, t), :] + o)
        return c

    lax.fori_loop(0, nt // 2, park, 0)
    lax.fori_loop(nt // 2, nt, complete, 0)


def _gla(gq, gk, gv, la, gr, wst, mf, mb, bdm, bdn, g, nl, t):
    b, s, _ = gq.shape
    seq = lambda w_: pl.BlockSpec((1, s, w_), lambda bi: (bi, 0, 0))
    const = lambda a: pl.BlockSpec(a.shape, lambda bi: (0,) * a.ndim)
    return pl.pallas_call(
        functools.partial(_gla_kernel, nl, t),
        grid=(b,),
        in_specs=[seq(GLA_QK_WIDTH), seq(GLA_QK_WIDTH), seq(GLA_WIDTH), seq(2 * GLA_QK_WIDTH), seq(GLA_WIDTH),
                  const(wst), const(mf), const(mb), const(bdm), const(bdn), const(g)],
        out_specs=seq(GLA_WIDTH),
        out_shape=jax.ShapeDtypeStruct((b, s, GLA_WIDTH), BF16),
        scratch_shapes=[pltpu.VMEM((2, GLA_WIDTH, GLA_QK_WIDTH), F32), pltpu.VMEM((s, GLA_WIDTH), F32)],
        compiler_params=_cparams("parallel"),
        name="gla_scan",
    )(gq, gk, gv, la, gr, wst, mf, mb, bdm, bdn, g)


def _dft_kernel(fc_ref, fs_ref, zc_ref, zs_ref, o_ref):
    o_ref[...] = (jnp.dot(fc_ref[...], zc_ref[...], preferred_element_type=F32)
                  + jnp.dot(fs_ref[...], zs_ref[...], preferred_element_type=F32)).astype(BF16)


def _position_dft(fc, fs, zc, zs, tm, tn):
    s, w = zc.shape
    return pl.pallas_call(
        _dft_kernel,
        grid=(w // tn, s // tm),
        in_specs=[pl.BlockSpec((tm, s), lambda j, i: (i, 0)),
                  pl.BlockSpec((tm, s), lambda j, i: (i, 0)),
                  pl.BlockSpec((s, tn), lambda j, i: (0, j)),
                  pl.BlockSpec((s, tn), lambda j, i: (0, j))],
        out_specs=pl.BlockSpec((tm, tn), lambda j, i: (i, j)),
        out_shape=jax.ShapeDtypeStruct((s, w), BF16),
        compiler_params=_cparams("parallel", "parallel"),
        name="position_dft",
    )(fc, fs, zc, zs)


def _outproj_kernel(alpha, n_exp, oatt_ref, ogla_ref, offt_ref, h_ref, wo_ref, g_ref, b_ref,
                    wrh_ref, wrl_ref, rb_ref, us_ref,
                    h1_ref, idx_ref, wts_ref, rank_ref, cnt_ref, carry_ref):
    i = pl.program_id(0)

    @pl.when(i == 0)
    def _():
        carry_ref[...] = jnp.zeros_like(carry_ref)

    acc = jnp.dot(oatt_ref[...], wo_ref[0:ATT_WIDTH, :], preferred_element_type=F32)
    acc = acc + jnp.dot(ogla_ref[...], wo_ref[ATT_WIDTH:ATT_WIDTH + GLA_WIDTH, :], preferred_element_type=F32)
    acc = acc + jnp.dot(offt_ref[...], wo_ref[ATT_WIDTH + GLA_WIDTH:, :], preferred_element_type=F32)
    h1 = _layer_norm_rows(alpha * h_ref[...] + acc, g_ref[...], b_ref[...])
    h1_ref[...] = h1

    hh = h1.astype(BF16)
    hl = (h1 - hh.astype(F32)).astype(BF16)
    wrh = wrh_ref[...]
    by_hh = lax.dot_general(jnp.concatenate([wrh, wrl_ref[...]], axis=0), hh, NT_DIMS, preferred_element_type=F32)
    logits = (by_hh[:n_exp] + by_hh[n_exp:]
              + lax.dot_general(wrh, hl, NT_DIMS, preferred_element_type=F32)) + rb_ref[...]
    tm = logits.shape[1]
    e_iota = lax.broadcasted_iota(jnp.int32, (n_exp, tm), 0)
    cur = logits
    vals, idxs, sels = [], [], []
    for _k in range(TOP_K):
        m = jnp.max(cur, axis=0, keepdims=True)
        ik = jnp.min(jnp.where(cur == m, e_iota, n_exp), axis=0, keepdims=True)
        sel = e_iota == ik
        vals.append(m)
        idxs.append(ik)
        sels.append(sel)
        cur = jnp.where(sel, -jnp.inf, cur)
    ex = [jnp.exp(v - vals[0]) for v in vals]
    den = ex[0] + ex[1] + ex[2] + ex[3]
    zero = jnp.zeros_like(den)
    idx_ref[...] = jnp.concatenate(idxs, axis=0)
    wts_ref[...] = jnp.concatenate([e / den for e in ex] + [zero] * (wts_ref.shape[0] - TOP_K), axis=0)

    onehot = jnp.zeros((n_exp, tm), F32)
    for sel in sels:
        onehot = onehot + sel.astype(F32)
    before = jnp.dot(onehot.astype(BF16), us_ref[...], preferred_element_type=F32) + carry_ref[:, 0:1]
    ranks = [jnp.sum(jnp.where(sel, before, 0.0), axis=0, keepdims=True) for sel in sels]
    rank_ref[...] = jnp.concatenate(ranks, axis=0).astype(jnp.int32)
    carry = carry_ref[...] + jnp.sum(onehot, axis=1, keepdims=True)
    carry_ref[...] = carry
    cnt_ref[...] = carry.astype(jnp.int32)


def _outproj(alpha, o_att, o_gla, o_fft, h, wo, g, b, wrh, wrl, rb, us, seq, tm):
    n, d = h.shape
    n_exp = wrh.shape[0]
    nst = seq // tm
    row = lambda w_: pl.BlockSpec((tm, w_), lambda i: (i, 0))
    const = lambda a: pl.BlockSpec(a.shape, lambda i: (0,) * a.ndim)
    tok = lambda r: pl.BlockSpec((r, tm), lambda i: (0, i))
    return pl.pallas_call(
        functools.partial(_outproj_kernel, alpha, n_exp),
        grid=(n // tm,),
        in_specs=[row(ATT_WIDTH), row(GLA_WIDTH),
                  pl.BlockSpec((tm, FNET_WIDTH), lambda i: (i % nst, i // nst)),
                  row(d), const(wo), const(g), const(b), const(wrh), const(wrl), const(rb), const(us)],
        out_specs=[row(d), tok(TOP_K), tok(2 * TOP_K), tok(TOP_K),
                   pl.BlockSpec((n_exp, LANES), lambda i: (0, 0))],
        out_shape=[jax.ShapeDtypeStruct((n, d), F32),
                   jax.ShapeDtypeStruct((TOP_K, n), jnp.int32),
                   jax.ShapeDtypeStruct((2 * TOP_K, n), F32),
                   jax.ShapeDtypeStruct((TOP_K, n), jnp.int32),
                   jax.ShapeDtypeStruct((n_exp, LANES), jnp.int32)],
        scratch_shapes=[pltpu.VMEM((n_exp, LANES), F32)],
        compiler_params=_cparams("arbitrary"),
        name="out_proj_router",
    )(o_att, o_gla, o_fft, h, wo, g, b, wrh, wrl, rb, us)


DMA_UNROLL = 8
ROW_SUB = 8


def _row_copy(src, dst, sem):
    return pltpu.make_async_copy(src, dst, sem)


def _dispatch_kernel(tb, nb, dest_ref, h_ref, xs_in_ref, xs_ref, buf, sem):
    del xs_in_ref
    i = pl.program_id(0)
    slot = i % 2
    buf[slot] = h_ref[...].reshape(buf.shape[1:])

    def start(t, c):
        for k in range(TOP_K):
            _row_copy(buf.at[slot, t], xs_ref.at[dest_ref[k, t]], sem.at[slot]).start(priority=k % 2)
        return c

    lax.fori_loop(0, tb, start, 0, unroll=DMA_UNROLL)

    def wait_block(s):
        def wait(t, c):
            for k in range(TOP_K):
                _row_copy(buf.at[s, 0], xs_ref.at[0], sem.at[s]).wait()
            return c
        lax.fori_loop(0, tb, wait, 0, unroll=DMA_UNROLL)

    @pl.when(i > 0)
    def _():
        wait_block(1 - slot)

    @pl.when(i == nb - 1)
    def _():
        wait_block(slot)


def _dispatch(dest, h, xs_prev, tb):
    n, d = h.shape
    return pl.pallas_call(
        functools.partial(_dispatch_kernel, tb, n // tb),
        grid=(n // tb,),
        in_specs=[pl.BlockSpec((TOP_K, tb), lambda i: (0, i), memory_space=pltpu.SMEM),
                  pl.BlockSpec((tb, d), lambda i: (i, 0)),
                  pl.BlockSpec(memory_space=pl.ANY)],
        out_specs=pl.BlockSpec(memory_space=pl.ANY),
        out_shape=jax.ShapeDtypeStruct(xs_prev.shape, F32),
        scratch_shapes=[pltpu.VMEM((2, tb, ROW_SUB, d // ROW_SUB), F32), pltpu.SemaphoreType.DMA((2,))],
        input_output_aliases={2: 0},
        compiler_params=_cparams("arbitrary"),
        name="moe_dispatch",
    )(dest, h, xs_prev)


def _ffn_kernel(tmf, t0_ref, nt_ref, x_hbm, wgu_ref, bgu_ref, wd_ref, bd_ref, y_hbm,
                xbuf, ybuf, wgu_bf, wd_bf, xsem, ysem):
    e = pl.program_id(0)
    t0 = t0_ref[e]
    nt = nt_ref[e]

    def x_copy(tile, slot):
        return pltpu.make_async_copy(x_hbm.at[pl.ds(tile * tmf, tmf)], xbuf.at[slot], xsem.at[slot])

    def y_copy(tile, slot):
        return pltpu.make_async_copy(ybuf.at[slot], y_hbm.at[pl.ds(tile * tmf, tmf)], ysem.at[slot])

    @pl.when(nt > 0)
    def _():
        x_copy(t0, 0).start()
        wgu_bf[...] = wgu_ref[0, 0].astype(BF16)
        wd_bf[...] = wd_ref[0, 0].astype(BF16)
        de, d = wd_bf.shape

        def body(t, c):
            slot = t % 2
            x_copy(t0 + t, slot).wait()

            @pl.when(t + 1 < nt)
            def _():
                x_copy(t0 + t + 1, 1 - slot).start()

            @pl.when(t >= 2)
            def _():
                y_copy(t0 + t - 2, slot).wait()

            x = xbuf[slot].reshape(tmf, d).astype(BF16)
            gu = jnp.dot(x, wgu_bf[...], preferred_element_type=F32) + bgu_ref[0, 0]
            gate = jnp.minimum(gu[:, :de], SWIGLU_LIMIT)
            up = jnp.clip(gu[:, de:], -SWIGLU_LIMIT, SWIGLU_LIMIT)
            hid = (up + 1.0) * (gate / (1.0 + jnp.exp(-SWIGLU_ALPHA * gate)))
            y = jnp.dot(hid.astype(BF16), wd_bf[...], preferred_element_type=F32) + bd_ref[0, 0]
            ybuf[slot] = y.reshape(ybuf.shape[1:])
            y_copy(t0 + t, slot).start()
            return c

        lax.fori_loop(0, nt, body, 0)

        @pl.when(nt >= 2)
        def _():
            y_copy(t0 + nt - 2, nt % 2).wait()
        y_copy(t0 + nt - 1, (nt - 1) % 2).wait()


def _expert_ffn(layer, tile_start, tile_count, xs, wgu, bgu, wd, bd, tmf):
    depth, n_exp, d, de2 = wgu.shape
    de = de2 // 2
    wspec = lambda r, c: pl.BlockSpec((1, 1, r, c), lambda e, t0, nt: (layer, e, 0, 0))
    hbm = pl.BlockSpec(memory_space=pl.ANY)
    tile = (2, tmf) + xs.shape[1:]
    grid_spec = pltpu.PrefetchScalarGridSpec(
        num_scalar_prefetch=2,
        grid=(n_exp,),
        in_specs=[hbm, wspec(d, de2), wspec(1, de2), wspec(de, d), wspec(1, d)],
        out_specs=hbm,
        scratch_shapes=[pltpu.VMEM(tile, F32), pltpu.VMEM(tile, F32),
                        pltpu.VMEM((d, de2), BF16), pltpu.VMEM((de, d), BF16),
                        pltpu.SemaphoreType.DMA((2,)), pltpu.SemaphoreType.DMA((2,))],
    )
    return pl.pallas_call(
        functools.partial(_ffn_kernel, tmf),
        grid_spec=grid_spec,
        out_shape=jax.ShapeDtypeStruct(xs.shape, F32),
        compiler_params=_cparams("arbitrary"),
        name="moe_expert_ffn",
    )(tile_start, tile_count, xs, wgu, bgu.reshape(depth, n_exp, 1, de2), wd, bd.reshape(depth, n_exp, 1, d))


def _combine_kernel(alpha, tb, nb, dcur_ref, dnxt_ref, w_ref, h1_ref, g_ref, b_ref, y_ref, o_ref, ybuf, sem):
    i = pl.program_id(0)
    slot = i % 2
    nslot = 1 - slot
    d = h1_ref.shape[1]

    def start_block(dref, s):
        def start(t, c):
            for k in range(TOP_K):
                _row_copy(y_ref.at[dref[k, t]], ybuf.at[s, k, t], sem.at[s]).start(priority=k % 2)
            return c
        lax.fori_loop(0, tb, start, 0, unroll=DMA_UNROLL)

    def wait_block(s):
        def wait(t, c):
            for k in range(TOP_K):
                _row_copy(y_ref.at[0], ybuf.at[s, k, 0], sem.at[s]).wait()
            return c
        lax.fori_loop(0, tb, wait, 0, unroll=DMA_UNROLL)

    @pl.when(i == 0)
    def _():
        start_block(dcur_ref, slot)

    @pl.when(i + 1 < nb)
    def _():
        start_block(dnxt_ref, nslot)

    wait_block(slot)
    wt = jnp.transpose(w_ref[...])
    acc = alpha * h1_ref[...]
    for k in range(TOP_K):
        acc = acc + ybuf[slot, k].reshape(tb, d) * wt[:, k:k + 1]
    o_ref[...] = _layer_norm_rows(acc, g_ref[...], b_ref[...])


def _combine(alpha, dest, wts, h1, g, b, ys, tb):
    n, d = h1.shape
    nb = n // tb
    dspec = lambda f: pl.BlockSpec((TOP_K, tb), f, memory_space=pltpu.SMEM)
    return pl.pallas_call(
        functools.partial(_combine_kernel, alpha, tb, nb),
        grid=(nb,),
        in_specs=[dspec(lambda i: (0, i)),
                  dspec(lambda i: (0, jnp.minimum(i + 1, nb - 1))),
                  pl.BlockSpec((2 * TOP_K, tb), lambda i: (0, i)),
                  pl.BlockSpec((tb, d), lambda i: (i, 0)),
                  pl.BlockSpec((1, d), lambda i: (0, 0)),
                  pl.BlockSpec((1, d), lambda i: (0, 0)),
                  pl.BlockSpec(memory_space=pl.ANY)],
        out_specs=pl.BlockSpec((tb, d), lambda i: (i, 0)),
        out_shape=jax.ShapeDtypeStruct((n, d), F32),
        scratch_shapes=[pltpu.VMEM((2, TOP_K, tb) + ys.shape[1:], F32),
                        pltpu.SemaphoreType.DMA((2,))],
        compiler_params=_cparams("arbitrary"),
        name="moe_combine",
    )(dest, dest, wts, h1, g.reshape(1, d), b.reshape(1, d), ys)


def _rope_tables(seq):
    pos = jnp.arange(seq)
    row_id = (pos // GRID_W).astype(F32)
    col_id = (pos % GRID_W).astype(F32)
    inv_freq = 1.0 / (ROPE_THETA ** (jnp.arange(ROPE_FREQS, dtype=F32) / ROPE_FREQS))
    lane = np.arange(LANES)
    hd = lane % HEAD_DIM
    freq = hd % ROPE_FREQS
    use_col = (hd // (2 * ROPE_FREQS)) == 1
    second = (hd % (2 * ROPE_FREQS)) >= ROPE_FREQS
    ang = jnp.where(use_col[None, :], col_id[:, None], row_id[:, None]) * inv_freq[freq][None, :]
    sign = np.where(second, 1.0, -1.0).astype(np.float32)
    return jnp.cos(ang), jnp.sin(ang) * sign[None, :]


def _dft_tables(seq):
    a = _pick(FNET_GROUP_DIM, seq)
    t = jnp.arange(seq, dtype=jnp.int32)

    def cos_sin(rows):
        ang = ((rows[:, None] * t[None, :]) % seq).astype(F32) * (2.0 * math.pi / seq)
        return jnp.cos(ang), jnp.sin(ang)

    cq, sq = [v[:, None, :] for v in cos_sin(jnp.arange(seq // a, dtype=jnp.int32) * a)]
    cr, sr = [v[None, :, :] for v in cos_sin(jnp.arange(a, dtype=jnp.int32))]
    scale = seq ** -0.5
    fc = ((cq * cr - sq * sr) * scale).reshape(seq, seq).astype(BF16)
    fs = ((sq * cr + cq * sr) * -scale).reshape(seq, seq).astype(BF16)
    c = np.arange(FNET_GROUP_DIM)
    angc = 2.0 * np.pi * ((c[:, None] * c[None, :]) % FNET_GROUP_DIM) / FNET_GROUP_DIM
    eye = np.eye(FNET_GROUPS)
    cc = np.kron(eye, np.cos(angc)) * FNET_GROUP_DIM ** -0.5
    sc = np.kron(eye, np.sin(angc)) * FNET_GROUP_DIM ** -0.5
    return fc, fs, jnp.asarray(cc, BF16), jnp.asarray(sc, BF16)


def _block_ones(width, group):
    g = np.arange(width) // group
    return jnp.asarray((g[:, None] == g[None, :]).astype(np.float32), BF16)


def _pick(limit, n):
    t = min(limit, n)
    while n % t:
        t //= 2
    return t


def kernel(x, ln_in_g, ln_in_b, w_in, att_q_gain, att_k_gain, gla_gate_w, gla_gate_b, gla_norm_g, w_out,
           ln1_g, ln1_b, router_w, router_b, exp_w_gu, exp_b_gu, exp_w_down, exp_b_down, ln2_g, ln2_b):
    batch, seq, d = x.shape
    n = batch * seq
    depth = w_in.shape[0]
    n_exp = router_w.shape[-1]
    alpha = (2.0 * depth) ** 0.25

    tm = _pick(512, seq)
    tmi = _pick(1024, seq)
    tq = _pick(512, seq)
    tg = _pick(GLA_TILE, seq // 2)
    tmf = _pick(512, n)
    tb = _pick(512, n)
    n_tiles = (TOP_K * n) // tmf + n_exp
    rows = n_tiles * tmf

    cos, sin = _rope_tables(seq)
    fc, fs, cc, sc = _dft_tables(seq)
    bd_head = _block_ones(LANES, HEAD_DIM)
    bd_gla = _block_ones(GLA_WIDTH, GLA_DV)
    vt_pad = jnp.zeros((batch, ATT_KV_HEADS, ATT_V_ROWS - HEAD_DIM, seq), BF16).at[:, :, 0, :].set(1.0)
    nl, wst, mf, mb = _gla_constants(tg)
    wst = jnp.asarray(wst, BF16)
    mf = jnp.asarray(mf)
    mb = jnp.asarray(mb)
    hv = np.arange(GLA_WIDTH) // GLA_DV
    hk = np.arange(GLA_QK_WIDTH) // GLA_DK
    bdm = jnp.asarray((hv[:, None] == hk[None, :]).astype(np.float32))
    us = np.arange(tm)
    us = jnp.asarray((us[:, None] < us[None, :]).astype(np.float32), BF16)

    w_pad = _pad_in_proj(w_in)
    qg = jnp.tile(att_q_gain, (1, LANES // HEAD_DIM)).reshape(depth, 1, LANES)
    kg = jnp.tile(att_k_gain, (1, LANES // HEAD_DIM)).reshape(depth, 1, LANES)
    gm = jnp.zeros((depth, LANES, 2 * GLA_QK_WIDTH), F32)
    gm = gm.at[:, :GLA_GATE_RANK, :GLA_QK_WIDTH].set(gla_gate_w[:, 0])
    gm = gm.at[:, GLA_GATE_RANK:2 * GLA_GATE_RANK, GLA_QK_WIDTH:].set(gla_gate_w[:, 1]).astype(BF16)
    gb = gla_gate_b.reshape(depth, 1, 2 * GLA_QK_WIDTH)
    gn = jnp.tile(gla_norm_g, (1, GLA_HEADS)).reshape(depth, 1, GLA_WIDTH)
    wo = w_out.astype(BF16)
    wr_t = jnp.swapaxes(router_w, 1, 2)
    wrh = wr_t.astype(BF16)
    wrl = (wr_t - wrh.astype(F32)).astype(BF16)
    rb = router_b.reshape(depth, n_exp, 1)

    e_ids = jnp.arange(n_exp, dtype=jnp.int32)
    xs = jnp.zeros((rows, ROW_SUB, d // ROW_SUB), F32)

    h = _layer_norm(x.reshape(n, d), ln_in_g, ln_in_b, tm)
    for l in range(depth):
        q, k2, av, gq, gk, gv, gr, la, zc, zs = _proj(
            h, w_pad[l], cos, sin, qg[l], kg[l], bd_head, gm[l], gb[l], cc, sc, batch, seq, tmi)
        vt = jnp.transpose(av.reshape(batch, seq, ATT_KV_HEADS, HEAD_DIM), (0, 2, 3, 1))
        vt = jnp.concatenate([vt, vt_pad], axis=2)
        o_att = _attention(q.reshape(batch, seq, -1), k2.reshape(batch, seq, -1), vt, tq)
        o_gla = _gla(gq.reshape(batch, seq, -1), gk.reshape(batch, seq, -1), gv.reshape(batch, seq, -1),
                     la.reshape(batch, seq, -1), gr.reshape(batch, seq, -1), wst, mf, mb, bdm, bd_gla, gn[l], nl, tg)
        o_fft = _position_dft(fc, fs, zc, zs, tm, _pick(512, batch * FNET_WIDTH))
        h1, idx, wts, rank, cnt = _outproj(
            alpha, o_att.reshape(n, -1), o_gla.reshape(n, -1), o_fft, h, wo[l], ln1_g[l].reshape(1, d),
            ln1_b[l].reshape(1, d), wrh[l], wrl[l], rb[l], us, seq, tm)

        counts = cnt[:, 0]
        tiles_per = (counts + tmf - 1) // tmf
        tile_end = jnp.cumsum(tiles_per)
        offs = (tile_end - tiles_per) * tmf
        chosen = idx[:, :, None] == e_ids[None, None, :]
        dest = rank + jnp.sum(jnp.where(chosen, offs[None, None, :], 0), axis=-1)

        xs = _dispatch(dest, h1, xs, tb)
        ys = _expert_ffn(l, (tile_end - tiles_per).astype(jnp.int32), tiles_per.astype(jnp.int32), xs,
                         exp_w_gu, exp_b_gu, exp_w_down, exp_b_down, tmf)
        h = _combine(alpha, dest, wts, h1, ln2_g[l], ln2_b[l], ys, tb)
    return h.reshape(batch, seq, d)
```

```python
import functools
import math

import numpy as np
import jax
import jax.numpy as jnp
from jax import lax
from jax.experimental import pallas as pl
from jax.experimental.pallas import tpu as pltpu

F32 = jnp.float32
BF16 = jnp.bfloat16

GRID_W = 64
HEAD_DIM = 64
ATT_HEADS = 8
ATT_KV_HEADS = 2
ATT_WIDTH = ATT_HEADS * HEAD_DIM
ATT_KV_WIDTH = ATT_KV_HEADS * HEAD_DIM
ROPE_THETA = 10000.0
ROPE_FREQS = HEAD_DIM // 4
GLA_HEADS = 4
GLA_DK = 32
GLA_DV = 64
GLA_QK_WIDTH = GLA_HEADS * GLA_DK
GLA_WIDTH = GLA_HEADS * GLA_DV
GLA_GATE_RANK = 16
GLA_TAU = 16.0
FNET_GROUPS = 4
FNET_GROUP_DIM = 64
FNET_WIDTH = FNET_GROUPS * FNET_GROUP_DIM
TOP_K = 4
SWIGLU_LIMIT = 7.0
SWIGLU_ALPHA = 1.702
LN_EPS = 1e-5
RMS_EPS = 1e-6

LANES = 128
VMEM_LIMIT_BYTES = 56 * 1024 * 1024

NT_DIMS = (((1,), (1,)), ((), ()))
TN_DIMS = (((0,), (0,)), ((), ()))


def _cparams(*sem):
    return pltpu.CompilerParams(dimension_semantics=sem, vmem_limit_bytes=VMEM_LIMIT_BYTES)


def _layer_norm_rows(x, g, b):
    mu = jnp.mean(x, axis=-1, keepdims=True)
    xc = x - mu
    var = jnp.mean(xc * xc, axis=-1, keepdims=True)
    return xc * lax.rsqrt(var + LN_EPS) * g + b


def _ln_kernel(x_ref, g_ref, b_ref, o_ref):
    o_ref[...] = _layer_norm_rows(x_ref[...], g_ref[...], b_ref[...])


def _layer_norm(x, g, b, tm):
    n, d = x.shape
    return pl.pallas_call(
        _ln_kernel,
        grid=(n // tm,),
        in_specs=[pl.BlockSpec((tm, d), lambda i: (i, 0)),
                  pl.BlockSpec((1, d), lambda i: (0, 0)),
                  pl.BlockSpec((1, d), lambda i: (0, 0))],
        out_specs=pl.BlockSpec((tm, d), lambda i: (i, 0)),
        out_shape=jax.ShapeDtypeStruct((n, d), F32),
        compiler_params=_cparams("parallel"),
        name="ln_in",
    )(x, g.reshape(1, d), b.reshape(1, d))


P_Q = 0
P_K = P_Q + ATT_WIDTH
P_V = P_K + 2 * ATT_KV_WIDTH
P_GQ = P_V + ATT_KV_WIDTH
P_GK = P_GQ + GLA_QK_WIDTH
P_GV = P_GK + GLA_QK_WIDTH
P_GR = P_GV + GLA_WIDTH
P_LR = P_GR + GLA_WIDTH
P_FX = P_LR + LANES
P_WIDTH = P_FX + FNET_WIDTH

Q_SCALE = HEAD_DIM ** -0.5 * math.log2(math.e)


def _pad_in_proj(w_in):
    l, d, _ = w_in.shape
    o = 0
    aq = w_in[..., o:o + ATT_WIDTH]; o += ATT_WIDTH
    ak = w_in[..., o:o + ATT_KV_WIDTH]; o += ATT_KV_WIDTH
    av = w_in[..., o:o + ATT_KV_WIDTH]; o += ATT_KV_WIDTH
    gq = w_in[..., o:o + GLA_QK_WIDTH]; o += GLA_QK_WIDTH
    gk = w_in[..., o:o + GLA_QK_WIDTH]; o += GLA_QK_WIDTH
    gv = w_in[..., o:o + GLA_WIDTH]; o += GLA_WIDTH
    gr = w_in[..., o:o + GLA_WIDTH]; o += GLA_WIDTH
    lr = w_in[..., o:o + 2 * GLA_GATE_RANK]; o += 2 * GLA_GATE_RANK
    fx = w_in[..., o:o + FNET_WIDTH]
    k0, k1 = ak[..., :HEAD_DIM], ak[..., HEAD_DIM:]
    zlr = jnp.zeros((l, d, LANES - 2 * GLA_GATE_RANK), w_in.dtype)
    cols = [aq, k0, k0, k1, k1, av, gq, gk, gv, gr, lr, zlr, fx]
    return jnp.concatenate(cols, axis=-1).astype(BF16)


def _proj_kernel(h_ref, w_ref, cos_ref, sin_ref, qg_ref, kg_ref, bd_ref, gm_ref, gb_ref,
                 cc_ref, sc_ref,
                 q_ref, k2_ref, v_ref, gq_ref, gk_ref, gv_ref, gr_ref, la_ref, zc_ref, zs_ref):
    acc = jnp.dot(h_ref[...].astype(BF16), w_ref[...], preferred_element_type=F32)
    cos = cos_ref[...]
    sin = sin_ref[...]
    bd = bd_ref[...]
    lane = lax.broadcasted_iota(jnp.int32, (1, LANES), 1)
    first_half = (lane % (2 * ROPE_FREQS)) < ROPE_FREQS

    def norm_rope(x, gain, scale):
        ss = jnp.dot((x * x).astype(BF16), bd, preferred_element_type=F32)
        xn = x * lax.rsqrt(ss * (1.0 / HEAD_DIM) + RMS_EPS) * gain
        partner = jnp.where(first_half, pltpu.roll(xn, LANES - ROPE_FREQS, 1), pltpu.roll(xn, ROPE_FREQS, 1))
        return (xn * cos + partner * sin) * scale

    for c in range(ATT_WIDTH // LANES):
        x = acc[:, P_Q + c * LANES:P_Q + (c + 1) * LANES]
        q_ref[:, c * LANES:(c + 1) * LANES] = norm_rope(x, qg_ref[...], Q_SCALE).astype(BF16)
    for c in range(2 * ATT_KV_WIDTH // LANES):
        x = acc[:, P_K + c * LANES:P_K + (c + 1) * LANES]
        k2_ref[:, c * LANES:(c + 1) * LANES] = norm_rope(x, kg_ref[...], 1.0).astype(BF16)
    v_ref[...] = acc[:, P_V:P_V + ATT_KV_WIDTH].astype(BF16)

    gq_ref[...] = (acc[:, P_GQ:P_GQ + GLA_QK_WIDTH] * (GLA_DK ** -0.5)).astype(BF16)
    gk_ref[...] = acc[:, P_GK:P_GK + GLA_QK_WIDTH].astype(BF16)
    gv_ref[...] = acc[:, P_GV:P_GV + GLA_WIDTH].astype(BF16)
    gr_ref[...] = acc[:, P_GR:P_GR + GLA_WIDTH].astype(BF16)
    z = jnp.dot(acc[:, P_LR:P_LR + LANES].astype(BF16), gm_ref[...], preferred_element_type=F32) + gb_ref[...]
    la_ref[...] = (jnp.minimum(z, 0.0) - jnp.log(1.0 + jnp.exp(-jnp.abs(z)))) * (1.0 / GLA_TAU)

    fx = acc[:, P_FX:P_FX + FNET_WIDTH].astype(BF16)
    zc_ref[...] = jnp.dot(fx, cc_ref[...], preferred_element_type=F32).astype(BF16)
    zs_ref[...] = jnp.dot(fx, sc_ref[...], preferred_element_type=F32).astype(BF16)


def _proj(h, w, cos, sin, qg, kg, bd, gm, gb, cc, sc, batch, seq, tm):
    n, d = h.shape
    nst = seq // tm
    row = lambda w_: pl.BlockSpec((tm, w_), lambda i: (i, 0))
    const = lambda a: pl.BlockSpec(a.shape, lambda i: (0,) * a.ndim)
    tab = pl.BlockSpec((tm, LANES), lambda i: (i % nst, 0))
    zspec = pl.BlockSpec((tm, FNET_WIDTH), lambda i: (i % nst, i // nst))
    outs = [(ATT_WIDTH, BF16), (2 * ATT_KV_WIDTH, BF16), (ATT_KV_WIDTH, BF16), (GLA_QK_WIDTH, BF16),
            (GLA_QK_WIDTH, BF16), (GLA_WIDTH, BF16), (GLA_WIDTH, BF16), (2 * GLA_QK_WIDTH, F32)]
    return pl.pallas_call(
        _proj_kernel,
        grid=(n // tm,),
        in_specs=[row(d), const(w), tab, tab, const(qg), const(kg), const(bd), const(gm),
                  const(gb), const(cc), const(sc)],
        out_specs=[row(w_) for w_, _ in outs] + [zspec, zspec],
        out_shape=[jax.ShapeDtypeStruct((n, w_), dt) for w_, dt in outs]
        + [jax.ShapeDtypeStruct((seq, batch * FNET_WIDTH), BF16)] * 2,
        compiler_params=_cparams("parallel"),
        name="in_proj",
    )(h, w, cos, sin, qg, kg, bd, gm, gb, cc, sc)


ATT_KEY_CHUNK = 512
ATT_V_ROWS = 128


def _attn_kernel(q_ref, k_ref, vt_ref, o_ref):
    s_len = k_ref.shape[1]
    kc = min(ATT_KEY_CHUNK, s_len)
    nkc = s_len // kc
    lane = lax.broadcasted_iota(jnp.int32, (1, LANES), 1)
    low = lane < HEAD_DIM
    outs = []
    for p in range(2):
        qp = q_ref[0, :, p * LANES:(p + 1) * LANES]
        qms = [jnp.where(low, qp, jnp.zeros_like(qp)), jnp.where(low, jnp.zeros_like(qp), qp)]

        def scores(c):
            kblk = k_ref[0, c * kc:(c + 1) * kc, :]
            return [lax.dot_general(kblk, qm, NT_DIMS, preferred_element_type=F32) for qm in qms]

        m = [None] * len(qms)
        acc = [None] * len(qms)
        s_next = scores(0)
        for c in range(nkc):
            s_cur = s_next
            if c + 1 < nkc:
                s_next = scores(c + 1)
            vt = vt_ref[0, 0, :, c * kc:(c + 1) * kc]
            for i in range(len(qms)):
                mc = jnp.max(s_cur[i], axis=0, keepdims=True)
                m_new = mc if c == 0 else jnp.maximum(m[i], mc)
                pe = jnp.exp2(s_cur[i] - m_new).astype(BF16)
                pv = jnp.dot(vt, pe, preferred_element_type=F32)
                acc[i] = pv if c == 0 else acc[i] * jnp.exp2(m[i] - m_new) + pv
                m[i] = m_new
        outs += [a[0:HEAD_DIM] / a[HEAD_DIM:HEAD_DIM + 1] for a in acc]
    o_ref[0] = jnp.transpose(jnp.concatenate(outs, axis=0)).astype(BF16)


def _attention(q, k2, vt, tq):
    b, s, _ = q.shape
    gw = ATT_WIDTH // ATT_KV_HEADS
    return pl.pallas_call(
        _attn_kernel,
        grid=(b, ATT_KV_HEADS, s // tq),
        in_specs=[pl.BlockSpec((1, tq, gw), lambda bi, g, qi: (bi, qi, g)),
                  pl.BlockSpec((1, s, LANES), lambda bi, g, qi: (bi, 0, g)),
                  pl.BlockSpec((1, 1, ATT_V_ROWS, s), lambda bi, g, qi: (bi, g, 0, 0))],
        out_specs=pl.BlockSpec((1, tq, gw), lambda bi, g, qi: (bi, qi, g)),
        out_shape=jax.ShapeDtypeStruct((b, s, ATT_WIDTH), BF16),
        compiler_params=_cparams("parallel", "parallel", "parallel"),
        name="gqa_attention",
    )(q, k2, vt)


GLA_TILE = 128


def _gla_constants(t):
    nl = int(math.log2(t))
    i = np.arange(t)[:, None]
    m = np.arange(t)[None, :]
    fwd = [m <= i, m > i]
    bwd = [m >= i, m < i]
    mask_f, mask_b = [], []
    for l in range(nl):
        s = t >> l
        half = s // 2
        blk = (i // s) * s
        ref = blk + half - 1
        right = (i - blk) >= half
        wf = np.where(right, (m > ref) & (m <= i), (m > i) & (m <= ref))
        wb = np.where(right, (m > ref) & (m < i), (m >= i) & (m <= ref))
        fwd.append(wf)
        bwd.append(wb)
        same = (i // s) == (m // s)
        mask_f.append(same)
        mask_b.append(same)
    mask_f.append(i == m)
    wst = np.stack([np.concatenate(fwd, 0), np.concatenate(bwd, 0)]).astype(np.float32)
    tile4 = lambda a: np.tile(a.astype(np.float32), (GLA_HEADS, 1))
    mf = np.stack([tile4(a) for a in mask_f])
    mb = np.stack([tile4(a) for a in mask_b])
    return nl, wst, mf, mb


def _gla_kernel(nl, t, q_ref, k_ref, v_ref, la_ref, r_ref, w_ref, mf_ref, mb_ref, bdm_ref, bdn_ref, g_ref,
                o_ref, st_ref, acc_ref):
    s = q_ref.shape[1]
    nt = s // t
    row = lax.broadcasted_iota(jnp.int32, (t, GLA_QK_WIDTH), 0)
    lane_k = lax.broadcasted_iota(jnp.int32, (1, GLA_QK_WIDTH), 1)
    lane_v = lax.broadcasted_iota(jnp.int32, (1, GLA_WIDTH), 1)
    head_k = [(lane_k // GLA_DK) == h for h in range(GLA_HEADS)]
    head_v = [(lane_v // GLA_DV) == h for h in range(GLA_HEADS)]

    dirs = (0, 1)

    def stack_heads(a):
        return jnp.concatenate([jnp.where(head_k[h], a, 0.0) for h in range(GLA_HEADS)], axis=0).astype(BF16)

    def tile_pair(t0s):
        q, k, v, x, o_inter, sc = [], [], [], [], [], []
        for d in dirs:
            t0 = t0s[d]
            q.append(q_ref[0, pl.ds(t0, t), :].astype(F32))
            k.append(k_ref[0, pl.ds(t0, t), :].astype(F32))
            v.append(v_ref[0, pl.ds(t0, t), :])
            la = la_ref[0, pl.ds(t0, t), d * GLA_QK_WIDTH:(d + 1) * GLA_QK_WIDTH]
            la_hi = la.astype(BF16)
            la_lo = (la - la_hi.astype(F32)).astype(BF16)
            wst = w_ref[d]
            x.append(jnp.exp(jnp.dot(wst, la_hi, preferred_element_type=F32)
                             + jnp.dot(wst, la_lo, preferred_element_type=F32)))
        for d in dirs:
            xq = x[d][0:t]
            q_st = (q[d] * xq).astype(BF16)
            k_st = (k[d] * x[d][t:2 * t]).astype(BF16)
            dec = xq[t - 1:t] if d == 0 else xq[0:1]
            st = st_ref[d]
            o_inter.append(lax.dot_general(q_st, st.astype(BF16), NT_DIMS, preferred_element_type=F32))
            kv = lax.dot_general(v[d], k_st, TN_DIMS, preferred_element_type=F32)
            st_ref[d] = st * dec + kv * bdm_ref[...]
            sc.append(jnp.zeros((GLA_HEADS * t, t), F32))
        for l in range(nl):
            half = t >> (l + 1)
            right = ((row // half) % 2) == 1
            for d in dirs:
                qside = right if d == 0 else jnp.logical_not(right)
                g = jnp.where(qside, q[d], k[d]) * x[d][(2 + l) * t:(3 + l) * t]
                a4 = stack_heads(jnp.where(qside, g, 0.0))
                b = jnp.where(qside, 0.0, g).astype(BF16)
                m_ref = mf_ref if d == 0 else mb_ref
                sc[d] = sc[d] + lax.dot_general(a4, b, NT_DIMS, preferred_element_type=F32) * m_ref[l]
        sc[0] = sc[0] + lax.dot_general(stack_heads(q[0]), k[0].astype(BF16), NT_DIMS,
                                        preferred_element_type=F32) * mf_ref[nl]
        outs = []
        for d in dirs:
            o4 = jnp.dot(sc[d].astype(BF16), v[d], preferred_element_type=F32)
            o = o_inter[d]
            for h in range(GLA_HEADS):
                o = o + jnp.where(head_v[h], o4[h * t:(h + 1) * t], 0.0)
            outs.append(o)
        return outs

    def finish(t0, tot):
        ss = jnp.dot((tot * tot).astype(BF16), bdn_ref[...], preferred_element_type=F32)
        r = r_ref[0, pl.ds(t0, t), :].astype(F32)
        y = tot * lax.rsqrt(ss * (1.0 / GLA_DV) + RMS_EPS) * g_ref[...] * (r / (1.0 + jnp.exp(-r)))
        o_ref[0, pl.ds(t0, t), :] = y.astype(BF16)

    def tiles(i):
        return pl.multiple_of(i * t, t), pl.multiple_of((nt - 1 - i) * t, t)

    st_ref[...] = jnp.zeros_like(st_ref)

    def park(i, c):
        t0s = tiles(i)
        for t0, o in zip(t0s, tile_pair(t0s)):
            acc_ref[pl.ds(t0, t), :] = o
        return c

    def complete(i, c):
        t0s = tiles(i)
        for t0, o in zip(t0s, tile_pair(t0s)):
            finish(t0, acc_ref[pl.ds(t0, t), :] + o)
        return c

    lax.fori_loop(0, nt // 2, park, 0)
    lax.fori_loop(nt // 2, nt, complete, 0)


def _gla(gq, gk, gv, la, gr, wst, mf, mb, bdm, bdn, g, nl, t):
    b, s, _ = gq.shape
    seq = lambda w_: pl.BlockSpec((1, s, w_), lambda bi: (bi, 0, 0))
    const = lambda a: pl.BlockSpec(a.shape, lambda bi: (0,) * a.ndim)
    return pl.pallas_call(
        functools.partial(_gla_kernel, nl, t),
        grid=(b,),
        in_specs=[seq(GLA_QK_WIDTH), seq(GLA_QK_WIDTH), seq(GLA_WIDTH), seq(2 * GLA_QK_WIDTH), seq(GLA_WIDTH),
                  const(wst), const(mf), const(mb), const(bdm), const(bdn), const(g)],
        out_specs=seq(GLA_WIDTH),
        out_shape=jax.ShapeDtypeStruct((b, s, GLA_WIDTH), BF16),
        scratch_shapes=[pltpu.VMEM((2, GLA_WIDTH, GLA_QK_WIDTH), F32), pltpu.VMEM((s, GLA_WIDTH), F32)],
        compiler_params=_cparams("parallel"),
        name="gla_scan",
    )(gq, gk, gv, la, gr, wst, mf, mb, bdm, bdn, g)


def _dft_kernel(fc_ref, fs_ref, zc_ref, zs_ref, o_ref):
    o_ref[...] = (jnp.dot(fc_ref[...], zc_ref[...], preferred_element_type=F32)
                  + jnp.dot(fs_ref[...], zs_ref[...], preferred_element_type=F32)).astype(BF16)


def _position_dft(fc, fs, zc, zs, tm, tn):
    s, w = zc.shape
    return pl.pallas_call(
        _dft_kernel,
        grid=(w // tn, s // tm),
        in_specs=[pl.BlockSpec((tm, s), lambda j, i: (i, 0)),
                  pl.BlockSpec((tm, s), lambda j, i: (i, 0)),
                  pl.BlockSpec((s, tn), lambda j, i: (0, j)),
                  pl.BlockSpec((s, tn), lambda j, i: (0, j))],
        out_specs=pl.BlockSpec((tm, tn), lambda j, i: (i, j)),
        out_shape=jax.ShapeDtypeStruct((s, w), BF16),
        compiler_params=_cparams("parallel", "parallel"),
        name="position_dft",
    )(fc, fs, zc, zs)


def _outproj_kernel(alpha, n_exp, oatt_ref, ogla_ref, offt_ref, h_ref, wo_ref, g_ref, b_ref,
                    wrh_ref, wrl_ref, rb_ref, us_ref,
                    h1_ref, idx_ref, wts_ref, rank_ref, cnt_ref, carry_ref):
    i = pl.program_id(0)

    @pl.when(i == 0)
    def _():
        carry_ref[...] = jnp.zeros_like(carry_ref)

    acc = jnp.dot(oatt_ref[...], wo_ref[0:ATT_WIDTH, :], preferred_element_type=F32)
    acc = acc + jnp.dot(ogla_ref[...], wo_ref[ATT_WIDTH:ATT_WIDTH + GLA_WIDTH, :], preferred_element_type=F32)
    acc = acc + jnp.dot(offt_ref[...], wo_ref[ATT_WIDTH + GLA_WIDTH:, :], preferred_element_type=F32)
    h1 = _layer_norm_rows(alpha * h_ref[...] + acc, g_ref[...], b_ref[...])
    h1_ref[...] = h1

    hh = h1.astype(BF16)
    hl = (h1 - hh.astype(F32)).astype(BF16)
    wrh = wrh_ref[...]
    by_hh = lax.dot_general(jnp.concatenate([wrh, wrl_ref[...]], axis=0), hh, NT_DIMS, preferred_element_type=F32)
    logits = (by_hh[:n_exp] + by_hh[n_exp:]
              + lax.dot_general(wrh, hl, NT_DIMS, preferred_element_type=F32)) + rb_ref[...]
    tm = logits.shape[1]
    e_iota = lax.broadcasted_iota(jnp.int32, (n_exp, tm), 0)
    cur = logits
    vals, idxs, sels = [], [], []
    for _k in range(TOP_K):
        m = jnp.max(cur, axis=0, keepdims=True)
        ik = jnp.min(jnp.where(cur == m, e_iota, n_exp), axis=0, keepdims=True)
        sel = e_iota == ik
        vals.append(m)
        idxs.append(ik)
        sels.append(sel)
        cur = jnp.where(sel, -jnp.inf, cur)
    ex = [jnp.exp(v - vals[0]) for v in vals]
    den = ex[0] + ex[1] + ex[2] + ex[3]
    zero = jnp.zeros_like(den)
    idx_ref[...] = jnp.concatenate(idxs, axis=0)
    wts_ref[...] = jnp.concatenate([e / den for e in ex] + [zero] * (wts_ref.shape[0] - TOP_K), axis=0)

    onehot = jnp.zeros((n_exp, tm), F32)
    for sel in sels:
        onehot = onehot + sel.astype(F32)
    before = jnp.dot(onehot.astype(BF16), us_ref[...], preferred_element_type=F32) + carry_ref[:, 0:1]
    ranks = [jnp.sum(jnp.where(sel, before, 0.0), axis=0, keepdims=True) for sel in sels]
    rank_ref[...] = jnp.concatenate(ranks, axis=0).astype(jnp.int32)
    carry = carry_ref[...] + jnp.sum(onehot, axis=1, keepdims=True)
    carry_ref[...] = carry
    cnt_ref[...] = carry.astype(jnp.int32)


def _outproj(alpha, o_att, o_gla, o_fft, h, wo, g, b, wrh, wrl, rb, us, seq, tm):
    n, d = h.shape
    n_exp = wrh.shape[0]
    nst = seq // tm
    row = lambda w_: pl.BlockSpec((tm, w_), lambda i: (i, 0))
    const = lambda a: pl.BlockSpec(a.shape, lambda i: (0,) * a.ndim)
    tok = lambda r: pl.BlockSpec((r, tm), lambda i: (0, i))
    return pl.pallas_call(
        functools.partial(_outproj_kernel, alpha, n_exp),
        grid=(n // tm,),
        in_specs=[row(ATT_WIDTH), row(GLA_WIDTH),
                  pl.BlockSpec((tm, FNET_WIDTH), lambda i: (i % nst, i // nst)),
                  row(d), const(wo), const(g), const(b), const(wrh), const(wrl), const(rb), const(us)],
        out_specs=[row(d), tok(TOP_K), tok(2 * TOP_K), tok(TOP_K),
                   pl.BlockSpec((n_exp, LANES), lambda i: (0, 0))],
        out_shape=[jax.ShapeDtypeStruct((n, d), F32),
                   jax.ShapeDtypeStruct((TOP_K, n), jnp.int32),
                   jax.ShapeDtypeStruct((2 * TOP_K, n), F32),
                   jax.ShapeDtypeStruct((TOP_K, n), jnp.int32),
                   jax.ShapeDtypeStruct((n_exp, LANES), jnp.int32)],
        scratch_shapes=[pltpu.VMEM((n_exp, LANES), F32)],
        compiler_params=_cparams("arbitrary"),
        name="out_proj_router",
    )(o_att, o_gla, o_fft, h, wo, g, b, wrh, wrl, rb, us)


DMA_UNROLL = 8
ROW_SUB = 8


def _row_copy(src, dst, sem):
    return pltpu.make_async_copy(src, dst, sem)


def _dispatch_kernel(tb, nb, dest_ref, h_ref, xs_in_ref, xs_ref, buf, sem):
    del xs_in_ref
    i = pl.program_id(0)
    slot = i % 2
    buf[slot] = h_ref[...].reshape(buf.shape[1:])

    def start(t, c):
        for k in range(TOP_K):
            _row_copy(buf.at[slot, t], xs_ref.at[dest_ref[k, t]], sem.at[slot]).start(priority=k % 2)
        return c

    lax.fori_loop(0, tb, start, 0, unroll=DMA_UNROLL)

    def wait_block(s):
        def wait(t, c):
            for k in range(TOP_K):
                _row_copy(buf.at[s, 0], xs_ref.at[0], sem.at[s]).wait()
            return c
        lax.fori_loop(0, tb, wait, 0, unroll=DMA_UNROLL)

    @pl.when(i > 0)
    def _():
        wait_block(1 - slot)

    @pl.when(i == nb - 1)
    def _():
        wait_block(slot)


def _dispatch(dest, h, xs_prev, tb):
    n, d = h.shape
    return pl.pallas_call(
        functools.partial(_dispatch_kernel, tb, n // tb),
        grid=(n // tb,),
        in_specs=[pl.BlockSpec((TOP_K, tb), lambda i: (0, i), memory_space=pltpu.SMEM),
                  pl.BlockSpec((tb, d), lambda i: (i, 0)),
                  pl.BlockSpec(memory_space=pl.ANY)],
        out_specs=pl.BlockSpec(memory_space=pl.ANY),
        out_shape=jax.ShapeDtypeStruct(xs_prev.shape, F32),
        scratch_shapes=[pltpu.VMEM((2, tb, ROW_SUB, d // ROW_SUB), F32), pltpu.SemaphoreType.DMA((2,))],
        input_output_aliases={2: 0},
        compiler_params=_cparams("arbitrary"),
        name="moe_dispatch",
    )(dest, h, xs_prev)


FFN_SUB_ROWS = 512


def _ffn_kernel(te_ref, tv_ref, x_ref, wgu_ref, bgu_ref, wd_ref, bd_ref, y_ref, wgu_bf, wd_bf):
    j = pl.program_id(0)
    valid = tv_ref[j]
    new_expert = jnp.logical_or(j == 0, te_ref[j] != te_ref[jnp.maximum(j - 1, 0)])

    @pl.when(jnp.logical_and(new_expert, valid > 0))
    def _():
        wgu_bf[...] = wgu_ref[0, 0].astype(BF16)
        wd_bf[...] = wd_ref[0, 0].astype(BF16)

    de, d = wd_bf.shape
    sub = min(FFN_SUB_ROWS, x_ref.shape[0])
    for r0 in range(0, x_ref.shape[0], sub):
        @pl.when(valid > r0)
        def _():
            x = x_ref[r0:r0 + sub].reshape(sub, d).astype(BF16)
            gu = jnp.dot(x, wgu_bf[...], preferred_element_type=F32) + bgu_ref[0, 0]
            gate = jnp.minimum(gu[:, :de], SWIGLU_LIMIT)
            up = jnp.clip(gu[:, de:], -SWIGLU_LIMIT, SWIGLU_LIMIT)
            hid = (up + 1.0) * (gate / (1.0 + jnp.exp(-SWIGLU_ALPHA * gate)))
            y = jnp.dot(hid.astype(BF16), wd_bf[...], preferred_element_type=F32) + bd_ref[0, 0]
            y_ref[r0:r0 + sub] = y.reshape((sub,) + y_ref.shape[1:])

        @pl.when(valid <= r0)
        def _():
            y_ref[r0:r0 + sub] = jnp.zeros((sub,) + y_ref.shape[1:], y_ref.dtype)


def _expert_ffn(layer, tile_expert, tile_valid, xs, wgu, bgu, wd, bd, tmf):
    rows = xs.shape[0]
    depth, n_exp, d, de2 = wgu.shape
    de = de2 // 2
    xspec = pl.BlockSpec((tmf,) + xs.shape[1:], lambda j, te, tv: (j, 0, 0))
    wspec = lambda r, c: pl.BlockSpec((1, 1, r, c), lambda j, te, tv: (layer, te[j], 0, 0))
    grid_spec = pltpu.PrefetchScalarGridSpec(
        num_scalar_prefetch=2,
        grid=(rows // tmf,),
        in_specs=[xspec, wspec(d, de2), wspec(1, de2), wspec(de, d), wspec(1, d)],
        out_specs=xspec,
        scratch_shapes=[pltpu.VMEM((d, de2), BF16), pltpu.VMEM((de, d), BF16)],
    )
    return pl.pallas_call(
        _ffn_kernel,
        grid_spec=grid_spec,
        out_shape=jax.ShapeDtypeStruct(xs.shape, F32),
        compiler_params=_cparams("arbitrary"),
        name="moe_expert_ffn",
    )(tile_expert, tile_valid, xs, wgu, bgu.reshape(depth, n_exp, 1, de2), wd, bd.reshape(depth, n_exp, 1, d))


def _combine_kernel(alpha, tb, nb, dcur_ref, dnxt_ref, w_ref, h1_ref, g_ref, b_ref, y_ref, o_ref, ybuf, sem):
    i = pl.program_id(0)
    slot = i % 2
    nslot = 1 - slot
    d = h1_ref.shape[1]

    def start_block(dref, s):
        def start(t, c):
            for k in range(TOP_K):
                _row_copy(y_ref.at[dref[k, t]], ybuf.at[s, k, t], sem.at[s]).start(priority=k % 2)
            return c
        lax.fori_loop(0, tb, start, 0, unroll=DMA_UNROLL)

    def wait_block(s):
        def wait(t, c):
            for k in range(TOP_K):
                _row_copy(y_ref.at[0], ybuf.at[s, k, 0], sem.at[s]).wait()
            return c
        lax.fori_loop(0, tb, wait, 0, unroll=DMA_UNROLL)

    @pl.when(i == 0)
    def _():
        start_block(dcur_ref, slot)

    @pl.when(i + 1 < nb)
    def _():
        start_block(dnxt_ref, nslot)

    wait_block(slot)
    wt = jnp.transpose(w_ref[...])
    acc = alpha * h1_ref[...]
    for k in range(TOP_K):
        acc = acc + ybuf[slot, k].reshape(tb, d) * wt[:, k:k + 1]
    o_ref[...] = _layer_norm_rows(acc, g_ref[...], b_ref[...])


def _combine(alpha, dest, wts, h1, g, b, ys, tb):
    n, d = h1.shape
    nb = n // tb
    dspec = lambda f: pl.BlockSpec((TOP_K, tb), f, memory_space=pltpu.SMEM)
    return pl.pallas_call(
        functools.partial(_combine_kernel, alpha, tb, nb),
        grid=(nb,),
        in_specs=[dspec(lambda i: (0, i)),
                  dspec(lambda i: (0, jnp.minimum(i + 1, nb - 1))),
                  pl.BlockSpec((2 * TOP_K, tb), lambda i: (0, i)),
                  pl.BlockSpec((tb, d), lambda i: (i, 0)),
                  pl.BlockSpec((1, d), lambda i: (0, 0)),
                  pl.BlockSpec((1, d), lambda i: (0, 0)),
                  pl.BlockSpec(memory_space=pl.ANY)],
        out_specs=pl.BlockSpec((tb, d), lambda i: (i, 0)),
        out_shape=jax.ShapeDtypeStruct((n, d), F32),
        scratch_shapes=[pltpu.VMEM((2, TOP_K, tb) + ys.shape[1:], F32),
                        pltpu.SemaphoreType.DMA((2,))],
        compiler_params=_cparams("arbitrary"),
        name="moe_combine",
    )(dest, dest, wts, h1, g.reshape(1, d), b.reshape(1, d), ys)


def _rope_tables(seq):
    pos = jnp.arange(seq)
    row_id = (pos // GRID_W).astype(F32)
    col_id = (pos % GRID_W).astype(F32)
    inv_freq = 1.0 / (ROPE_THETA ** (jnp.arange(ROPE_FREQS, dtype=F32) / ROPE_FREQS))
    lane = np.arange(LANES)
    hd = lane % HEAD_DIM
    freq = hd % ROPE_FREQS
    use_col = (hd // (2 * ROPE_FREQS)) == 1
    second = (hd % (2 * ROPE_FREQS)) >= ROPE_FREQS
    ang = jnp.where(use_col[None, :], col_id[:, None], row_id[:, None]) * inv_freq[freq][None, :]
    sign = np.where(second, 1.0, -1.0).astype(np.float32)
    return jnp.cos(ang), jnp.sin(ang) * sign[None, :]


def _dft_tables(seq):
    a = _pick(FNET_GROUP_DIM, seq)
    t = jnp.arange(seq, dtype=jnp.int32)

    def cos_sin(rows):
        ang = ((rows[:, None] * t[None, :]) % seq).astype(F32) * (2.0 * math.pi / seq)
        return jnp.cos(ang), jnp.sin(ang)

    cq, sq = [v[:, None, :] for v in cos_sin(jnp.arange(seq // a, dtype=jnp.int32) * a)]
    cr, sr = [v[None, :, :] for v in cos_sin(jnp.arange(a, dtype=jnp.int32))]
    scale = seq ** -0.5
    fc = ((cq * cr - sq * sr) * scale).reshape(seq, seq).astype(BF16)
    fs = ((sq * cr + cq * sr) * -scale).reshape(seq, seq).astype(BF16)
    c = np.arange(FNET_GROUP_DIM)
    angc = 2.0 * np.pi * ((c[:, None] * c[None, :]) % FNET_GROUP_DIM) / FNET_GROUP_DIM
    eye = np.eye(FNET_GROUPS)
    cc = np.kron(eye, np.cos(angc)) * FNET_GROUP_DIM ** -0.5
    sc = np.kron(eye, np.sin(angc)) * FNET_GROUP_DIM ** -0.5
    return fc, fs, jnp.asarray(cc, BF16), jnp.asarray(sc, BF16)


def _block_ones(width, group):
    g = np.arange(width) // group
    return jnp.asarray((g[:, None] == g[None, :]).astype(np.float32), BF16)


def _pick(limit, n):
    t = min(limit, n)
    while n % t:
        t //= 2
    return t


def kernel(x, ln_in_g, ln_in_b, w_in, att_q_gain, att_k_gain, gla_gate_w, gla_gate_b, gla_norm_g, w_out,
           ln1_g, ln1_b, router_w, router_b, exp_w_gu, exp_b_gu, exp_w_down, exp_b_down, ln2_g, ln2_b):
    batch, seq, d = x.shape
    n = batch * seq
    depth = w_in.shape[0]
    n_exp = router_w.shape[-1]
    alpha = (2.0 * depth) ** 0.25

    tm = _pick(512, seq)
    tmi = _pick(1024, seq)
    tq = _pick(1024, seq)
    tg = _pick(GLA_TILE, seq // 2)
    tmf = _pick(1024, n)
    tb = _pick(512, n)
    n_tiles = (TOP_K * n) // tmf + n_exp
    rows = n_tiles * tmf

    cos, sin = _rope_tables(seq)
    fc, fs, cc, sc = _dft_tables(seq)
    bd_head = _block_ones(LANES, HEAD_DIM)
    bd_gla = _block_ones(GLA_WIDTH, GLA_DV)
    vt_pad = jnp.zeros((batch, ATT_KV_HEADS, ATT_V_ROWS - HEAD_DIM, seq), BF16).at[:, :, 0, :].set(1.0)
    nl, wst, mf, mb = _gla_constants(tg)
    wst = jnp.asarray(wst, BF16)
    mf = jnp.asarray(mf)
    mb = jnp.asarray(mb)
    hv = np.arange(GLA_WIDTH) // GLA_DV
    hk = np.arange(GLA_QK_WIDTH) // GLA_DK
    bdm = jnp.asarray((hv[:, None] == hk[None, :]).astype(np.float32))
    us = np.arange(tm)
    us = jnp.asarray((us[:, None] < us[None, :]).astype(np.float32), BF16)

    w_pad = _pad_in_proj(w_in)
    qg = jnp.tile(att_q_gain, (1, LANES // HEAD_DIM)).reshape(depth, 1, LANES)
    kg = jnp.tile(att_k_gain, (1, LANES // HEAD_DIM)).reshape(depth, 1, LANES)
    gm = jnp.zeros((depth, LANES, 2 * GLA_QK_WIDTH), F32)
    gm = gm.at[:, :GLA_GATE_RANK, :GLA_QK_WIDTH].set(gla_gate_w[:, 0])
    gm = gm.at[:, GLA_GATE_RANK:2 * GLA_GATE_RANK, GLA_QK_WIDTH:].set(gla_gate_w[:, 1]).astype(BF16)
    gb = gla_gate_b.reshape(depth, 1, 2 * GLA_QK_WIDTH)
    gn = jnp.tile(gla_norm_g, (1, GLA_HEADS)).reshape(depth, 1, GLA_WIDTH)
    wo = w_out.astype(BF16)
    wr_t = jnp.swapaxes(router_w, 1, 2)
    wrh = wr_t.astype(BF16)
    wrl = (wr_t - wrh.astype(F32)).astype(BF16)
    rb = router_b.reshape(depth, n_exp, 1)

    e_ids = jnp.arange(n_exp, dtype=jnp.int32)
    tile_ids = jnp.arange(n_tiles, dtype=jnp.int32)
    xs = jnp.zeros((rows, ROW_SUB, d // ROW_SUB), F32)

    h = _layer_norm(x.reshape(n, d), ln_in_g, ln_in_b, tm)
    for l in range(depth):
        q, k2, av, gq, gk, gv, gr, la, zc, zs = _proj(
            h, w_pad[l], cos, sin, qg[l], kg[l], bd_head, gm[l], gb[l], cc, sc, batch, seq, tmi)
        vt = jnp.transpose(av.reshape(batch, seq, ATT_KV_HEADS, HEAD_DIM), (0, 2, 3, 1))
        vt = jnp.concatenate([vt, vt_pad], axis=2)
        o_att = _attention(q.reshape(batch, seq, -1), k2.reshape(batch, seq, -1), vt, tq)
        o_gla = _gla(gq.reshape(batch, seq, -1), gk.reshape(batch, seq, -1), gv.reshape(batch, seq, -1),
                     la.reshape(batch, seq, -1), gr.reshape(batch, seq, -1), wst, mf, mb, bdm, bd_gla, gn[l], nl, tg)
        o_fft = _position_dft(fc, fs, zc, zs, tm, _pick(512, batch * FNET_WIDTH))
        h1, idx, wts, rank, cnt = _outproj(
            alpha, o_att.reshape(n, -1), o_gla.reshape(n, -1), o_fft, h, wo[l], ln1_g[l].reshape(1, d),
            ln1_b[l].reshape(1, d), wrh[l], wrl[l], rb[l], us, seq, tm)

        counts = cnt[:, 0]
        tiles_per = (counts + tmf - 1) // tmf
        tile_end = jnp.cumsum(tiles_per)
        offs = (tile_end - tiles_per) * tmf
        chosen = idx[:, :, None] == e_ids[None, None, :]
        dest = rank + jnp.sum(jnp.where(chosen, offs[None, None, :], 0), axis=-1)
        used = tile_end[-1]
        tile_of = jnp.minimum(tile_ids, used - 1)
        tile_expert = jnp.sum((tile_end[None, :] <= tile_of[:, None]).astype(jnp.int32), axis=1)
        own = tile_expert[:, None] == e_ids[None, :]
        group_end = jnp.sum(jnp.where(own, (offs + counts)[None, :], 0), axis=1)
        tile_valid = jnp.where(tile_ids < used, jnp.clip(group_end - tile_ids * tmf, 0, tmf), 0).astype(jnp.int32)

        xs = _dispatch(dest, h1, xs, tb)
        ys = _expert_ffn(l, tile_expert, tile_valid, xs, exp_w_gu, exp_b_gu, exp_w_down, exp_b_down, tmf)
        h = _combine(alpha, dest, wts, h1, ln2_g[l], ln2_b[l], ys, tb)
    return h.reshape(batch, seq, d)
```

```python
import functools
import math

import numpy as np
import jax
import jax.numpy as jnp
from jax import lax
from jax.experimental import pallas as pl
from jax.experimental.pallas import tpu as pltpu

F32 = jnp.float32
BF16 = jnp.bfloat16

GRID_W = 64
HEAD_DIM = 64
ATT_HEADS = 8
ATT_KV_HEADS = 2
ATT_WIDTH = ATT_HEADS * HEAD_DIM
ATT_KV_WIDTH = ATT_KV_HEADS * HEAD_DIM
ROPE_THETA = 10000.0
ROPE_FREQS = HEAD_DIM // 4
GLA_HEADS = 4
GLA_DK = 32
GLA_DV = 64
GLA_QK_WIDTH = GLA_HEADS * GLA_DK
GLA_WIDTH = GLA_HEADS * GLA_DV
GLA_GATE_RANK = 16
GLA_TAU = 16.0
FNET_GROUPS = 4
FNET_GROUP_DIM = 64
FNET_WIDTH = FNET_GROUPS * FNET_GROUP_DIM
TOP_K = 4
SWIGLU_LIMIT = 7.0
SWIGLU_ALPHA = 1.702
LN_EPS = 1e-5
RMS_EPS = 1e-6

LANES = 128
VMEM_LIMIT_BYTES = 56 * 1024 * 1024

NT_DIMS = (((1,), (1,)), ((), ()))
TN_DIMS = (((0,), (0,)), ((), ()))


def _cparams(*sem):
    return pltpu.CompilerParams(dimension_semantics=sem, vmem_limit_bytes=VMEM_LIMIT_BYTES)


def _layer_norm_rows(x, g, b):
    mu = jnp.mean(x, axis=-1, keepdims=True)
    xc = x - mu
    var = jnp.mean(xc * xc, axis=-1, keepdims=True)
    return xc * lax.rsqrt(var + LN_EPS) * g + b


def _ln_kernel(x_ref, g_ref, b_ref, o_ref):
    o_ref[...] = _layer_norm_rows(x_ref[...], g_ref[...], b_ref[...])


def _layer_norm(x, g, b, tm):
    n, d = x.shape
    return pl.pallas_call(
        _ln_kernel,
        grid=(n // tm,),
        in_specs=[pl.BlockSpec((tm, d), lambda i: (i, 0)),
                  pl.BlockSpec((1, d), lambda i: (0, 0)),
                  pl.BlockSpec((1, d), lambda i: (0, 0))],
        out_specs=pl.BlockSpec((tm, d), lambda i: (i, 0)),
        out_shape=jax.ShapeDtypeStruct((n, d), F32),
        compiler_params=_cparams("parallel"),
        name="ln_in",
    )(x, g.reshape(1, d), b.reshape(1, d))


P_Q = 0
P_K = P_Q + ATT_WIDTH
P_V = P_K + 2 * ATT_KV_WIDTH
P_GQ = P_V + ATT_KV_WIDTH
P_GK = P_GQ + GLA_QK_WIDTH
P_GV = P_GK + GLA_QK_WIDTH
P_GR = P_GV + GLA_WIDTH
P_LR = P_GR + GLA_WIDTH
P_FX = P_LR + LANES
P_WIDTH = P_FX + FNET_WIDTH

Q_SCALE = HEAD_DIM ** -0.5 * math.log2(math.e)


def _pad_in_proj(w_in):
    l, d, _ = w_in.shape
    o = 0
    aq = w_in[..., o:o + ATT_WIDTH]; o += ATT_WIDTH
    ak = w_in[..., o:o + ATT_KV_WIDTH]; o += ATT_KV_WIDTH
    av = w_in[..., o:o + ATT_KV_WIDTH]; o += ATT_KV_WIDTH
    gq = w_in[..., o:o + GLA_QK_WIDTH]; o += GLA_QK_WIDTH
    gk = w_in[..., o:o + GLA_QK_WIDTH]; o += GLA_QK_WIDTH
    gv = w_in[..., o:o + GLA_WIDTH]; o += GLA_WIDTH
    gr = w_in[..., o:o + GLA_WIDTH]; o += GLA_WIDTH
    lr = w_in[..., o:o + 2 * GLA_GATE_RANK]; o += 2 * GLA_GATE_RANK
    fx = w_in[..., o:o + FNET_WIDTH]
    k0, k1 = ak[..., :HEAD_DIM], ak[..., HEAD_DIM:]
    zlr = jnp.zeros((l, d, LANES - 2 * GLA_GATE_RANK), w_in.dtype)
    cols = [aq, k0, k0, k1, k1, av, gq, gk, gv, gr, lr, zlr, fx]
    return jnp.concatenate(cols, axis=-1).astype(BF16)


def _proj_kernel(h_ref, w_ref, cos_ref, sin_ref, qg_ref, kg_ref, bd_ref, gm_ref, gb_ref,
                 cc_ref, sc_ref,
                 q_ref, k2_ref, v_ref, gq_ref, gk_ref, gv_ref, gr_ref, la_ref, zc_ref, zs_ref):
    acc = jnp.dot(h_ref[...].astype(BF16), w_ref[...], preferred_element_type=F32)
    cos = cos_ref[...]
    sin = sin_ref[...]
    bd = bd_ref[...]
    lane = lax.broadcasted_iota(jnp.int32, (1, LANES), 1)
    first_half = (lane % (2 * ROPE_FREQS)) < ROPE_FREQS

    def norm_rope(x, gain, scale):
        ss = jnp.dot((x * x).astype(BF16), bd, preferred_element_type=F32)
        xn = x * lax.rsqrt(ss * (1.0 / HEAD_DIM) + RMS_EPS) * gain
        partner = jnp.where(first_half, pltpu.roll(xn, LANES - ROPE_FREQS, 1), pltpu.roll(xn, ROPE_FREQS, 1))
        return (xn * cos + partner * sin) * scale

    for c in range(ATT_WIDTH // LANES):
        x = acc[:, P_Q + c * LANES:P_Q + (c + 1) * LANES]
        q_ref[:, c * LANES:(c + 1) * LANES] = norm_rope(x, qg_ref[...], Q_SCALE).astype(BF16)
    for c in range(2 * ATT_KV_WIDTH // LANES):
        x = acc[:, P_K + c * LANES:P_K + (c + 1) * LANES]
        k2_ref[:, c * LANES:(c + 1) * LANES] = norm_rope(x, kg_ref[...], 1.0).astype(BF16)
    v_ref[...] = acc[:, P_V:P_V + ATT_KV_WIDTH].astype(BF16)

    gq_ref[...] = (acc[:, P_GQ:P_GQ + GLA_QK_WIDTH] * (GLA_DK ** -0.5)).astype(BF16)
    gk_ref[...] = acc[:, P_GK:P_GK + GLA_QK_WIDTH].astype(BF16)
    gv_ref[...] = acc[:, P_GV:P_GV + GLA_WIDTH].astype(BF16)
    gr_ref[...] = acc[:, P_GR:P_GR + GLA_WIDTH].astype(BF16)
    z = jnp.dot(acc[:, P_LR:P_LR + LANES].astype(BF16), gm_ref[...], preferred_element_type=F32) + gb_ref[...]
    la_ref[...] = (jnp.minimum(z, 0.0) - jnp.log(1.0 + jnp.exp(-jnp.abs(z)))) * (1.0 / GLA_TAU)

    fx = acc[:, P_FX:P_FX + FNET_WIDTH].astype(BF16)
    zc_ref[...] = jnp.dot(fx, cc_ref[...], preferred_element_type=F32).astype(BF16)
    zs_ref[...] = jnp.dot(fx, sc_ref[...], preferred_element_type=F32).astype(BF16)


def _proj(h, w, cos, sin, qg, kg, bd, gm, gb, cc, sc, batch, seq, tm):
    n, d = h.shape
    nst = seq // tm
    row = lambda w_: pl.BlockSpec((tm, w_), lambda i: (i, 0))
    const = lambda a: pl.BlockSpec(a.shape, lambda i: (0,) * a.ndim)
    tab = pl.BlockSpec((tm, LANES), lambda i: (i % nst, 0))
    zspec = pl.BlockSpec((tm, FNET_WIDTH), lambda i: (i % nst, i // nst))
    outs = [(ATT_WIDTH, BF16), (2 * ATT_KV_WIDTH, BF16), (ATT_KV_WIDTH, BF16), (GLA_QK_WIDTH, BF16),
            (GLA_QK_WIDTH, BF16), (GLA_WIDTH, BF16), (GLA_WIDTH, BF16), (2 * GLA_QK_WIDTH, F32)]
    return pl.pallas_call(
        _proj_kernel,
        grid=(n // tm,),
        in_specs=[row(d), const(w), tab, tab, const(qg), const(kg), const(bd), const(gm),
                  const(gb), const(cc), const(sc)],
        out_specs=[row(w_) for w_, _ in outs] + [zspec, zspec],
        out_shape=[jax.ShapeDtypeStruct((n, w_), dt) for w_, dt in outs]
        + [jax.ShapeDtypeStruct((seq, batch * FNET_WIDTH), BF16)] * 2,
        compiler_params=_cparams("parallel"),
        name="in_proj",
    )(h, w, cos, sin, qg, kg, bd, gm, gb, cc, sc)


ATT_KEY_CHUNK = 512
ATT_V_ROWS = 128


def _attn_kernel(q_ref, k_ref, vt_ref, o_ref):
    s_len = k_ref.shape[1]
    kc = min(ATT_KEY_CHUNK, s_len)
    nkc = s_len // kc
    lane = lax.broadcasted_iota(jnp.int32, (1, LANES), 1)
    low = lane < HEAD_DIM
    outs = []
    for p in range(2):
        qp = q_ref[0, :, p * LANES:(p + 1) * LANES]
        qms = [jnp.where(low, qp, jnp.zeros_like(qp)), jnp.where(low, jnp.zeros_like(qp), qp)]

        def scores(c):
            kblk = k_ref[0, c * kc:(c + 1) * kc, :]
            return [lax.dot_general(kblk, qm, NT_DIMS, preferred_element_type=F32) for qm in qms]

        m = [None] * len(qms)
        acc = [None] * len(qms)
        s_next = scores(0)
        for c in range(nkc):
            s_cur = s_next
            if c + 1 < nkc:
                s_next = scores(c + 1)
            vt = vt_ref[0, 0, :, c * kc:(c + 1) * kc]
            for i in range(len(qms)):
                mc = jnp.max(s_cur[i], axis=0, keepdims=True)
                m_new = mc if c == 0 else jnp.maximum(m[i], mc)
                pe = jnp.exp2(s_cur[i] - m_new).astype(BF16)
                pv = jnp.dot(vt, pe, preferred_element_type=F32)
                acc[i] = pv if c == 0 else acc[i] * jnp.exp2(m[i] - m_new) + pv
                m[i] = m_new
        outs += [a[0:HEAD_DIM] / a[HEAD_DIM:HEAD_DIM + 1] for a in acc]
    o_ref[0] = jnp.transpose(jnp.concatenate(outs, axis=0)).astype(BF16)


def _attention(q, k2, vt, tq):
    b, s, _ = q.shape
    gw = ATT_WIDTH // ATT_KV_HEADS
    return pl.pallas_call(
        _attn_kernel,
        grid=(b, ATT_KV_HEADS, s // tq),
        in_specs=[pl.BlockSpec((1, tq, gw), lambda bi, g, qi: (bi, qi, g)),
                  pl.BlockSpec((1, s, LANES), lambda bi, g, qi: (bi, 0, g)),
                  pl.BlockSpec((1, 1, ATT_V_ROWS, s), lambda bi, g, qi: (bi, g, 0, 0))],
        out_specs=pl.BlockSpec((1, tq, gw), lambda bi, g, qi: (bi, qi, g)),
        out_shape=jax.ShapeDtypeStruct((b, s, ATT_WIDTH), BF16),
        compiler_params=_cparams("parallel", "parallel", "parallel"),
        name="gqa_attention",
    )(q, k2, vt)


GLA_TILE = 128


def _gla_constants(t):
    nl = int(math.log2(t))
    i = np.arange(t)[:, None]
    m = np.arange(t)[None, :]
    fwd = [m <= i, m > i]
    bwd = [m >= i, m < i]
    mask_f, mask_b = [], []
    for l in range(nl):
        s = t >> l
        half = s // 2
        blk = (i // s) * s
        ref = blk + half - 1
        right = (i - blk) >= half
        wf = np.where(right, (m > ref) & (m <= i), (m > i) & (m <= ref))
        wb = np.where(right, (m > ref) & (m < i), (m >= i) & (m <= ref))
        fwd.append(wf)
        bwd.append(wb)
        same = (i // s) == (m // s)
        mask_f.append(same)
        mask_b.append(same)
    mask_f.append(i == m)
    wst = np.stack([np.concatenate(fwd, 0), np.concatenate(bwd, 0)]).astype(np.float32)
    tile4 = lambda a: np.tile(a.astype(np.float32), (GLA_HEADS, 1))
    mf = np.stack([tile4(a) for a in mask_f])
    mb = np.stack([tile4(a) for a in mask_b])
    return nl, wst, mf, mb


def _gla_kernel(nl, t, q_ref, k_ref, v_ref, la_ref, r_ref, w_ref, mf_ref, mb_ref, bdm_ref, bdn_ref, g_ref,
                o_ref, st_ref, acc_ref):
    s = q_ref.shape[1]
    nt = s // t
    row = lax.broadcasted_iota(jnp.int32, (t, GLA_QK_WIDTH), 0)
    lane_k = lax.broadcasted_iota(jnp.int32, (1, GLA_QK_WIDTH), 1)
    lane_v = lax.broadcasted_iota(jnp.int32, (1, GLA_WIDTH), 1)
    head_k = [(lane_k // GLA_DK) == h for h in range(GLA_HEADS)]
    head_v = [(lane_v // GLA_DV) == h for h in range(GLA_HEADS)]

    dirs = (0, 1)

    def stack_heads(a):
        return jnp.concatenate([jnp.where(head_k[h], a, 0.0) for h in range(GLA_HEADS)], axis=0).astype(BF16)

    def tile_pair(t0s):
        q, k, v, x, o_inter, sc = [], [], [], [], [], []
        for d in dirs:
            t0 = t0s[d]
            q.append(q_ref[0, pl.ds(t0, t), :].astype(F32))
            k.append(k_ref[0, pl.ds(t0, t), :].astype(F32))
            v.append(v_ref[0, pl.ds(t0, t), :])
            la = la_ref[0, pl.ds(t0, t), d * GLA_QK_WIDTH:(d + 1) * GLA_QK_WIDTH]
            la_hi = la.astype(BF16)
            la_lo = (la - la_hi.astype(F32)).astype(BF16)
            wst = w_ref[d]
            x.append(jnp.exp(jnp.dot(wst, la_hi, preferred_element_type=F32)
                             + jnp.dot(wst, la_lo, preferred_element_type=F32)))
        for d in dirs:
            xq = x[d][0:t]
            q_st = (q[d] * xq).astype(BF16)
            k_st = (k[d] * x[d][t:2 * t]).astype(BF16)
            dec = xq[t - 1:t] if d == 0 else xq[0:1]
            st = st_ref[d]
            o_inter.append(lax.dot_general(q_st, st.astype(BF16), NT_DIMS, preferred_element_type=F32))
            kv = lax.dot_general(v[d], k_st, TN_DIMS, preferred_element_type=F32)
            st_ref[d] = st * dec + kv * bdm_ref[...]
            sc.append(jnp.zeros((GLA_HEADS * t, t), F32))
        for l in range(nl):
            half = t >> (l + 1)
            right = ((row // half) % 2) == 1
            for d in dirs:
                qside = right if d == 0 else jnp.logical_not(right)
                g = jnp.where(qside, q[d], k[d]) * x[d][(2 + l) * t:(3 + l) * t]
                a4 = stack_heads(jnp.where(qside, g, 0.0))
                b = jnp.where(qside, 0.0, g).astype(BF16)
                m_ref = mf_ref if d == 0 else mb_ref
                sc[d] = sc[d] + lax.dot_general(a4, b, NT_DIMS, preferred_element_type=F32) * m_ref[l]
        sc[0] = sc[0] + lax.dot_general(stack_heads(q[0]), k[0].astype(BF16), NT_DIMS,
                                        preferred_element_type=F32) * mf_ref[nl]
        outs = []
        for d in dirs:
            o4 = jnp.dot(sc[d].astype(BF16), v[d], preferred_element_type=F32)
            o = o_inter[d]
            for h in range(GLA_HEADS):
                o = o + jnp.where(head_v[h], o4[h * t:(h + 1) * t], 0.0)
            outs.append(o)
        return outs

    def finish(t0, tot):
        ss = jnp.dot((tot * tot).astype(BF16), bdn_ref[...], preferred_element_type=F32)
        r = r_ref[0, pl.ds(t0, t), :].astype(F32)
        y = tot * lax.rsqrt(ss * (1.0 / GLA_DV) + RMS_EPS) * g_ref[...] * (r / (1.0 + jnp.exp(-r)))
        o_ref[0, pl.ds(t0, t), :] = y.astype(BF16)

    def tiles(i):
        return pl.multiple_of(i * t, t), pl.multiple_of((nt - 1 - i) * t, t)

    st_ref[...] = jnp.zeros_like(st_ref)

    def park(i, c):
        t0s = tiles(i)
        for t0, o in zip(t0s, tile_pair(t0s)):
            acc_ref[pl.ds(t0, t), :] = o
        return c

    def complete(i, c):
        t0s = tiles(i)
        for t0, o in zip(t0s, tile_pair(t0s)):
            finish(t0, acc_ref[pl.ds(t0, t), :] + o)
        return c

    lax.fori_loop(0, nt // 2, park, 0)
    lax.fori_loop(nt // 2, nt, complete, 0)


def _gla(gq, gk, gv, la, gr, wst, mf, mb, bdm, bdn, g, nl, t):
    b, s, _ = gq.shape
    seq = lambda w_: pl.BlockSpec((1, s, w_), lambda bi: (bi, 0, 0))
    const = lambda a: pl.BlockSpec(a.shape, lambda bi: (0,) * a.ndim)
    return pl.pallas_call(
        functools.partial(_gla_kernel, nl, t),
        grid=(b,),
        in_specs=[seq(GLA_QK_WIDTH), seq(GLA_QK_WIDTH), seq(GLA_WIDTH), seq(2 * GLA_QK_WIDTH), seq(GLA_WIDTH),
                  const(wst), const(mf), const(mb), const(bdm), const(bdn), const(g)],
        out_specs=seq(GLA_WIDTH),
        out_shape=jax.ShapeDtypeStruct((b, s, GLA_WIDTH), BF16),
        scratch_shapes=[pltpu.VMEM((2, GLA_WIDTH, GLA_QK_WIDTH), F32), pltpu.VMEM((s, GLA_WIDTH), F32)],
        compiler_params=_cparams("parallel"),
        name="gla_scan",
    )(gq, gk, gv, la, gr, wst, mf, mb, bdm, bdn, g)


def _dft_kernel(fc_ref, fs_ref, zc_ref, zs_ref, o_ref):
    o_ref[...] = (jnp.dot(fc_ref[...], zc_ref[...], preferred_element_type=F32)
                  + jnp.dot(fs_ref[...], zs_ref[...], preferred_element_type=F32)).astype(BF16)


def _position_dft(fc, fs, zc, zs, tm, tn):
    s, w = zc.shape
    return pl.pallas_call(
        _dft_kernel,
        grid=(w // tn, s // tm),
        in_specs=[pl.BlockSpec((tm, s), lambda j, i: (i, 0)),
                  pl.BlockSpec((tm, s), lambda j, i: (i, 0)),
                  pl.BlockSpec((s, tn), lambda j, i: (0, j)),
                  pl.BlockSpec((s, tn), lambda j, i: (0, j))],
        out_specs=pl.BlockSpec((tm, tn), lambda j, i: (i, j)),
        out_shape=jax.ShapeDtypeStruct((s, w), BF16),
        compiler_params=_cparams("parallel", "parallel"),
        name="position_dft",
    )(fc, fs, zc, zs)


def _outproj_kernel(alpha, n_exp, oatt_ref, ogla_ref, offt_ref, h_ref, wo_ref, g_ref, b_ref,
                    wrh_ref, wrl_ref, rb_ref, us_ref,
                    h1_ref, idx_ref, wts_ref, rank_ref, cnt_ref, carry_ref):
    i = pl.program_id(0)

    @pl.when(i == 0)
    def _():
        carry_ref[...] = jnp.zeros_like(carry_ref)

    acc = jnp.dot(oatt_ref[...], wo_ref[0:ATT_WIDTH, :], preferred_element_type=F32)
    acc = acc + jnp.dot(ogla_ref[...], wo_ref[ATT_WIDTH:ATT_WIDTH + GLA_WIDTH, :], preferred_element_type=F32)
    acc = acc + jnp.dot(offt_ref[...], wo_ref[ATT_WIDTH + GLA_WIDTH:, :], preferred_element_type=F32)
    h1 = _layer_norm_rows(alpha * h_ref[...] + acc, g_ref[...], b_ref[...])
    h1_ref[...] = h1

    hh = h1.astype(BF16)
    hl = (h1 - hh.astype(F32)).astype(BF16)
    wrh = wrh_ref[...]
    by_hh = lax.dot_general(jnp.concatenate([wrh, wrl_ref[...]], axis=0), hh, NT_DIMS, preferred_element_type=F32)
    logits = (by_hh[:n_exp] + by_hh[n_exp:]
              + lax.dot_general(wrh, hl, NT_DIMS, preferred_element_type=F32)) + rb_ref[...]
    tm = logits.shape[1]
    e_iota = lax.broadcasted_iota(jnp.int32, (n_exp, tm), 0)
    cur = logits
    vals, idxs, sels = [], [], []
    for _k in range(TOP_K):
        m = jnp.max(cur, axis=0, keepdims=True)
        ik = jnp.min(jnp.where(cur == m, e_iota, n_exp), axis=0, keepdims=True)
        sel = e_iota == ik
        vals.append(m)
        idxs.append(ik)
        sels.append(sel)
        cur = jnp.where(sel, -jnp.inf, cur)
    ex = [jnp.exp(v - vals[0]) for v in vals]
    den = ex[0] + ex[1] + ex[2] + ex[3]
    zero = jnp.zeros_like(den)
    idx_ref[...] = jnp.concatenate(idxs, axis=0)
    wts_ref[...] = jnp.concatenate([e / den for e in ex] + [zero] * (wts_ref.shape[0] - TOP_K), axis=0)

    onehot = jnp.zeros((n_exp, tm), F32)
    for sel in sels:
        onehot = onehot + sel.astype(F32)
    before = jnp.dot(onehot.astype(BF16), us_ref[...], preferred_element_type=F32) + carry_ref[:, 0:1]
    ranks = [jnp.sum(jnp.where(sel, before, 0.0), axis=0, keepdims=True) for sel in sels]
    rank_ref[...] = jnp.concatenate(ranks, axis=0).astype(jnp.int32)
    carry = carry_ref[...] + jnp.sum(onehot, axis=1, keepdims=True)
    carry_ref[...] = carry
    cnt_ref[...] = carry.astype(jnp.int32)


def _outproj(alpha, o_att, o_gla, o_fft, h, wo, g, b, wrh, wrl, rb, us, seq, tm):
    n, d = h.shape
    n_exp = wrh.shape[0]
    nst = seq // tm
    row = lambda w_: pl.BlockSpec((tm, w_), lambda i: (i, 0))
    const = lambda a: pl.BlockSpec(a.shape, lambda i: (0,) * a.ndim)
    tok = lambda r: pl.BlockSpec((r, tm), lambda i: (0, i))
    return pl.pallas_call(
        functools.partial(_outproj_kernel, alpha, n_exp),
        grid=(n // tm,),
        in_specs=[row(ATT_WIDTH), row(GLA_WIDTH),
                  pl.BlockSpec((tm, FNET_WIDTH), lambda i: (i % nst, i // nst)),
                  row(d), const(wo), const(g), const(b), const(wrh), const(wrl), const(rb), const(us)],
        out_specs=[row(d), tok(TOP_K), tok(2 * TOP_K), tok(TOP_K),
                   pl.BlockSpec((n_exp, LANES), lambda i: (0, 0))],
        out_shape=[jax.ShapeDtypeStruct((n, d), F32),
                   jax.ShapeDtypeStruct((TOP_K, n), jnp.int32),
                   jax.ShapeDtypeStruct((2 * TOP_K, n), F32),
                   jax.ShapeDtypeStruct((TOP_K, n), jnp.int32),
                   jax.ShapeDtypeStruct((n_exp, LANES), jnp.int32)],
        scratch_shapes=[pltpu.VMEM((n_exp, LANES), F32)],
        compiler_params=_cparams("arbitrary"),
        name="out_proj_router",
    )(o_att, o_gla, o_fft, h, wo, g, b, wrh, wrl, rb, us)


DMA_UNROLL = 8
ROW_SUB = 8


def _row_copy(src, dst, sem):
    return pltpu.make_async_copy(src, dst, sem)


def _dispatch_kernel(tb, nb, dest_ref, h_ref, xs_in_ref, xs_ref, buf, sem):
    del xs_in_ref
    i = pl.program_id(0)
    slot = i % 2
    buf[slot] = h_ref[...].reshape(buf.shape[1:])

    def start(t, c):
        for k in range(TOP_K):
            _row_copy(buf.at[slot, t], xs_ref.at[dest_ref[k, t]], sem.at[slot]).start(priority=k % 2)
        return c

    lax.fori_loop(0, tb, start, 0, unroll=DMA_UNROLL)

    def wait_block(s):
        def wait(t, c):
            for k in range(TOP_K):
                _row_copy(buf.at[s, 0], xs_ref.at[0], sem.at[s]).wait()
            return c
        lax.fori_loop(0, tb, wait, 0, unroll=DMA_UNROLL)

    @pl.when(i > 0)
    def _():
        wait_block(1 - slot)

    @pl.when(i == nb - 1)
    def _():
        wait_block(slot)


def _dispatch(dest, h, xs_prev, tb):
    n, d = h.shape
    return pl.pallas_call(
        functools.partial(_dispatch_kernel, tb, n // tb),
        grid=(n // tb,),
        in_specs=[pl.BlockSpec((TOP_K, tb), lambda i: (0, i), memory_space=pltpu.SMEM),
                  pl.BlockSpec((tb, d), lambda i: (i, 0)),
                  pl.BlockSpec(memory_space=pl.ANY)],
        out_specs=pl.BlockSpec(memory_space=pl.ANY),
        out_shape=jax.ShapeDtypeStruct(xs_prev.shape, F32),
        scratch_shapes=[pltpu.VMEM((2, tb, ROW_SUB, d // ROW_SUB), F32), pltpu.SemaphoreType.DMA((2,))],
        input_output_aliases={2: 0},
        compiler_params=_cparams("arbitrary"),
        name="moe_dispatch",
    )(dest, h, xs_prev)


def _ffn_kernel(te_ref, tv_ref, tof_ref, x_ref, wgu_ref, bgu_ref, wd_ref, bd_ref, y_ref, wgu_bf, wd_bf):
    j = pl.program_id(0)
    valid = tv_ref[j]
    new_expert = jnp.logical_or(j == 0, te_ref[j] != te_ref[jnp.maximum(j - 1, 0)])

    @pl.when(jnp.logical_and(new_expert, valid > 0))
    def _():
        wgu_bf[...] = wgu_ref[0, 0].astype(BF16)
        wd_bf[...] = wd_ref[0, 0].astype(BF16)

    @pl.when(valid > 0)
    def _():
        de, d = wd_bf.shape
        x = x_ref[...].reshape(x_ref.shape[0], d).astype(BF16)
        gu = jnp.dot(x, wgu_bf[...], preferred_element_type=F32) + bgu_ref[0, 0]
        gate = jnp.minimum(gu[:, :de], SWIGLU_LIMIT)
        up = jnp.clip(gu[:, de:], -SWIGLU_LIMIT, SWIGLU_LIMIT)
        hid = (up + 1.0) * (gate / (1.0 + jnp.exp(-SWIGLU_ALPHA * gate)))
        y = jnp.dot(hid.astype(BF16), wd_bf[...], preferred_element_type=F32) + bd_ref[0, 0]
        y_ref[...] = y.reshape(y_ref.shape)

    @pl.when(valid == 0)
    def _():
        y_ref[...] = jnp.zeros_like(y_ref)


def _expert_ffn(layer, tile_expert, tile_valid, tile_of, xs, wgu, bgu, wd, bd, tmf):
    rows = xs.shape[0]
    depth, n_exp, d, de2 = wgu.shape
    de = de2 // 2
    xspec = pl.BlockSpec((tmf,) + xs.shape[1:], lambda j, te, tv, tof: (tof[j], 0, 0))
    yspec = pl.BlockSpec((tmf,) + xs.shape[1:], lambda j, te, tv, tof: (j, 0, 0))
    wspec = lambda r, c: pl.BlockSpec((1, 1, r, c), lambda j, te, tv, tof: (layer, te[j], 0, 0))
    grid_spec = pltpu.PrefetchScalarGridSpec(
        num_scalar_prefetch=3,
        grid=(rows // tmf,),
        in_specs=[xspec, wspec(d, de2), wspec(1, de2), wspec(de, d), wspec(1, d)],
        out_specs=yspec,
        scratch_shapes=[pltpu.VMEM((d, de2), BF16), pltpu.VMEM((de, d), BF16)],
    )
    return pl.pallas_call(
        _ffn_kernel,
        grid_spec=grid_spec,
        out_shape=jax.ShapeDtypeStruct(xs.shape, F32),
        compiler_params=_cparams("arbitrary"),
        name="moe_expert_ffn",
    )(tile_expert, tile_valid, tile_of, xs, wgu, bgu.reshape(depth, n_exp, 1, de2), wd,
      bd.reshape(depth, n_exp, 1, d))


def _combine_kernel(alpha, tb, nb, dcur_ref, dnxt_ref, w_ref, h1_ref, g_ref, b_ref, y_ref, o_ref, ybuf, sem):
    i = pl.program_id(0)
    slot = i % 2
    nslot = 1 - slot
    d = h1_ref.shape[1]

    def start_block(dref, s):
        def start(t, c):
            for k in range(TOP_K):
                _row_copy(y_ref.at[dref[k, t]], ybuf.at[s, k, t], sem.at[s]).start(priority=k % 2)
            return c
        lax.fori_loop(0, tb, start, 0, unroll=DMA_UNROLL)

    def wait_block(s):
        def wait(t, c):
            for k in range(TOP_K):
                _row_copy(y_ref.at[0], ybuf.at[s, k, 0], sem.at[s]).wait()
            return c
        lax.fori_loop(0, tb, wait, 0, unroll=DMA_UNROLL)

    @pl.when(i == 0)
    def _():
        start_block(dcur_ref, slot)

    @pl.when(i + 1 < nb)
    def _():
        start_block(dnxt_ref, nslot)

    wait_block(slot)
    wt = jnp.transpose(w_ref[...])
    acc = alpha * h1_ref[...]
    for k in range(TOP_K):
        acc = acc + ybuf[slot, k].reshape(tb, d) * wt[:, k:k + 1]
    o_ref[...] = _layer_norm_rows(acc, g_ref[...], b_ref[...])


def _combine(alpha, dest, wts, h1, g, b, ys, tb):
    n, d = h1.shape
    nb = n // tb
    dspec = lambda f: pl.BlockSpec((TOP_K, tb), f, memory_space=pltpu.SMEM)
    return pl.pallas_call(
        functools.partial(_combine_kernel, alpha, tb, nb),
        grid=(nb,),
        in_specs=[dspec(lambda i: (0, i)),
                  dspec(lambda i: (0, jnp.minimum(i + 1, nb - 1))),
                  pl.BlockSpec((2 * TOP_K, tb), lambda i: (0, i)),
                  pl.BlockSpec((tb, d), lambda i: (i, 0)),
                  pl.BlockSpec((1, d), lambda i: (0, 0)),
                  pl.BlockSpec((1, d), lambda i: (0, 0)),
                  pl.BlockSpec(memory_space=pl.ANY)],
        out_specs=pl.BlockSpec((tb, d), lambda i: (i, 0)),
        out_shape=jax.ShapeDtypeStruct((n, d), F32),
        scratch_shapes=[pltpu.VMEM((2, TOP_K, tb) + ys.shape[1:], F32),
                        pltpu.SemaphoreType.DMA((2,))],
        compiler_params=_cparams("arbitrary"),
        name="moe_combine",
    )(dest, dest, wts, h1, g.reshape(1, d), b.reshape(1, d), ys)


def _rope_tables(seq):
    pos = jnp.arange(seq)
    row_id = (pos // GRID_W).astype(F32)
    col_id = (pos % GRID_W).astype(F32)
    inv_freq = 1.0 / (ROPE_THETA ** (jnp.arange(ROPE_FREQS, dtype=F32) / ROPE_FREQS))
    lane = np.arange(LANES)
    hd = lane % HEAD_DIM
    freq = hd % ROPE_FREQS
    use_col = (hd // (2 * ROPE_FREQS)) == 1
    second = (hd % (2 * ROPE_FREQS)) >= ROPE_FREQS
    ang = jnp.where(use_col[None, :], col_id[:, None], row_id[:, None]) * inv_freq[freq][None, :]
    sign = np.where(second, 1.0, -1.0).astype(np.float32)
    return jnp.cos(ang), jnp.sin(ang) * sign[None, :]


def _dft_tables(seq):
    a = _pick(FNET_GROUP_DIM, seq)
    t = jnp.arange(seq, dtype=jnp.int32)

    def cos_sin(rows):
        ang = ((rows[:, None] * t[None, :]) % seq).astype(F32) * (2.0 * math.pi / seq)
        return jnp.cos(ang), jnp.sin(ang)

    cq, sq = [v[:, None, :] for v in cos_sin(jnp.arange(seq // a, dtype=jnp.int32) * a)]
    cr, sr = [v[None, :, :] for v in cos_sin(jnp.arange(a, dtype=jnp.int32))]
    scale = seq ** -0.5
    fc = ((cq * cr - sq * sr) * scale).reshape(seq, seq).astype(BF16)
    fs = ((sq * cr + cq * sr) * -scale).reshape(seq, seq).astype(BF16)
    c = np.arange(FNET_GROUP_DIM)
    angc = 2.0 * np.pi * ((c[:, None] * c[None, :]) % FNET_GROUP_DIM) / FNET_GROUP_DIM
    eye = np.eye(FNET_GROUPS)
    cc = np.kron(eye, np.cos(angc)) * FNET_GROUP_DIM ** -0.5
    sc = np.kron(eye, np.sin(angc)) * FNET_GROUP_DIM ** -0.5
    return fc, fs, jnp.asarray(cc, BF16), jnp.asarray(sc, BF16)


def _block_ones(width, group):
    g = np.arange(width) // group
    return jnp.asarray((g[:, None] == g[None, :]).astype(np.float32), BF16)


def _pick(limit, n):
    t = min(limit, n)
    while n % t:
        t //= 2
    return t


def kernel(x, ln_in_g, ln_in_b, w_in, att_q_gain, att_k_gain, gla_gate_w, gla_gate_b, gla_norm_g, w_out,
           ln1_g, ln1_b, router_w, router_b, exp_w_gu, exp_b_gu, exp_w_down, exp_b_down, ln2_g, ln2_b):
    batch, seq, d = x.shape
    n = batch * seq
    depth = w_in.shape[0]
    n_exp = router_w.shape[-1]
    alpha = (2.0 * depth) ** 0.25

    tm = _pick(512, seq)
    tmi = _pick(1024, seq)
    tq = _pick(1024, seq)
    tg = _pick(GLA_TILE, seq // 2)
    tmf = _pick(1024, n)
    tb = _pick(256, n)
    n_tiles = (TOP_K * n) // tmf + n_exp
    rows = n_tiles * tmf

    cos, sin = _rope_tables(seq)
    fc, fs, cc, sc = _dft_tables(seq)
    bd_head = _block_ones(LANES, HEAD_DIM)
    bd_gla = _block_ones(GLA_WIDTH, GLA_DV)
    vt_pad = jnp.zeros((batch, ATT_KV_HEADS, ATT_V_ROWS - HEAD_DIM, seq), BF16).at[:, :, 0, :].set(1.0)
    nl, wst, mf, mb = _gla_constants(tg)
    wst = jnp.asarray(wst, BF16)
    mf = jnp.asarray(mf)
    mb = jnp.asarray(mb)
    hv = np.arange(GLA_WIDTH) // GLA_DV
    hk = np.arange(GLA_QK_WIDTH) // GLA_DK
    bdm = jnp.asarray((hv[:, None] == hk[None, :]).astype(np.float32))
    us = np.arange(tm)
    us = jnp.asarray((us[:, None] < us[None, :]).astype(np.float32), BF16)

    w_pad = _pad_in_proj(w_in)
    qg = jnp.tile(att_q_gain, (1, LANES // HEAD_DIM)).reshape(depth, 1, LANES)
    kg = jnp.tile(att_k_gain, (1, LANES // HEAD_DIM)).reshape(depth, 1, LANES)
    gm = jnp.zeros((depth, LANES, 2 * GLA_QK_WIDTH), F32)
    gm = gm.at[:, :GLA_GATE_RANK, :GLA_QK_WIDTH].set(gla_gate_w[:, 0])
    gm = gm.at[:, GLA_GATE_RANK:2 * GLA_GATE_RANK, GLA_QK_WIDTH:].set(gla_gate_w[:, 1]).astype(BF16)
    gb = gla_gate_b.reshape(depth, 1, 2 * GLA_QK_WIDTH)
    gn = jnp.tile(gla_norm_g, (1, GLA_HEADS)).reshape(depth, 1, GLA_WIDTH)
    wo = w_out.astype(BF16)
    wr_t = jnp.swapaxes(router_w, 1, 2)
    wrh = wr_t.astype(BF16)
    wrl = (wr_t - wrh.astype(F32)).astype(BF16)
    rb = router_b.reshape(depth, n_exp, 1)

    e_ids = jnp.arange(n_exp, dtype=jnp.int32)
    tile_ids = jnp.arange(n_tiles, dtype=jnp.int32)
    xs = jnp.zeros((rows, ROW_SUB, d // ROW_SUB), F32)

    h = _layer_norm(x.reshape(n, d), ln_in_g, ln_in_b, tm)
    for l in range(depth):
        q, k2, av, gq, gk, gv, gr, la, zc, zs = _proj(
            h, w_pad[l], cos, sin, qg[l], kg[l], bd_head, gm[l], gb[l], cc, sc, batch, seq, tmi)
        vt = jnp.transpose(av.reshape(batch, seq, ATT_KV_HEADS, HEAD_DIM), (0, 2, 3, 1))
        vt = jnp.concatenate([vt, vt_pad], axis=2)
        o_att = _attention(q.reshape(batch, seq, -1), k2.reshape(batch, seq, -1), vt, tq)
        o_gla = _gla(gq.reshape(batch, seq, -1), gk.reshape(batch, seq, -1), gv.reshape(batch, seq, -1),
                     la.reshape(batch, seq, -1), gr.reshape(batch, seq, -1), wst, mf, mb, bdm, bd_gla, gn[l], nl, tg)
        o_fft = _position_dft(fc, fs, zc, zs, tm, _pick(512, batch * FNET_WIDTH))
        h1, idx, wts, rank, cnt = _outproj(
            alpha, o_att.reshape(n, -1), o_gla.reshape(n, -1), o_fft, h, wo[l], ln1_g[l].reshape(1, d),
            ln1_b[l].reshape(1, d), wrh[l], wrl[l], rb[l], us, seq, tm)

        counts = cnt[:, 0]
        tiles_per = (counts + tmf - 1) // tmf
        tile_end = jnp.cumsum(tiles_per)
        offs = (tile_end - tiles_per) * tmf
        chosen = idx[:, :, None] == e_ids[None, None, :]
        dest = rank + jnp.sum(jnp.where(chosen, offs[None, None, :], 0), axis=-1)
        used = tile_end[-1]
        tile_of = jnp.minimum(tile_ids, used - 1)
        tile_expert = jnp.sum((tile_end[None, :] <= tile_of[:, None]).astype(jnp.int32), axis=1)
        own = tile_expert[:, None] == e_ids[None, :]
        group_end = jnp.sum(jnp.where(own, (offs + counts)[None, :], 0), axis=1)
        tile_valid = jnp.where(tile_ids < used, jnp.clip(group_end - tile_ids * tmf, 0, tmf), 0).astype(jnp.int32)

        xs = _dispatch(dest, h1, xs, tb)
        ys = _expert_ffn(l, tile_expert, tile_valid, tile_of, xs, exp_w_gu, exp_b_gu, exp_w_down, exp_b_down, tmf)
        h = _combine(alpha, dest, wts, h1, ln2_g[l], ln2_b[l], ys, tb)
    return h.reshape(batch, seq, d)
```

```python
import functools
import math

import numpy as np
import jax
import jax.numpy as jnp
from jax import lax
from jax.experimental import pallas as pl
from jax.experimental.pallas import tpu as pltpu

F32 = jnp.float32
BF16 = jnp.bfloat16

GRID_W = 64
HEAD_DIM = 64
ATT_HEADS = 8
ATT_KV_HEADS = 2
ATT_WIDTH = ATT_HEADS * HEAD_DIM
ATT_KV_WIDTH = ATT_KV_HEADS * HEAD_DIM
ROPE_THETA = 10000.0
ROPE_FREQS = HEAD_DIM // 4
GLA_HEADS = 4
GLA_DK = 32
GLA_DV = 64
GLA_QK_WIDTH = GLA_HEADS * GLA_DK
GLA_WIDTH = GLA_HEADS * GLA_DV
GLA_GATE_RANK = 16
GLA_TAU = 16.0
FNET_GROUPS = 4
FNET_GROUP_DIM = 64
FNET_WIDTH = FNET_GROUPS * FNET_GROUP_DIM
TOP_K = 4
SWIGLU_LIMIT = 7.0
SWIGLU_ALPHA = 1.702
LN_EPS = 1e-5
RMS_EPS = 1e-6

LANES = 128
VMEM_LIMIT_BYTES = 56 * 1024 * 1024

NT_DIMS = (((1,), (1,)), ((), ()))
TN_DIMS = (((0,), (0,)), ((), ()))


def _cparams(*sem):
    return pltpu.CompilerParams(dimension_semantics=sem, vmem_limit_bytes=VMEM_LIMIT_BYTES)


def _layer_norm_rows(x, g, b):
    mu = jnp.mean(x, axis=-1, keepdims=True)
    xc = x - mu
    var = jnp.mean(xc * xc, axis=-1, keepdims=True)
    return xc * lax.rsqrt(var + LN_EPS) * g + b


def _ln_kernel(x_ref, g_ref, b_ref, o_ref):
    o_ref[...] = _layer_norm_rows(x_ref[...], g_ref[...], b_ref[...])


def _layer_norm(x, g, b, tm):
    n, d = x.shape
    return pl.pallas_call(
        _ln_kernel,
        grid=(n // tm,),
        in_specs=[pl.BlockSpec((tm, d), lambda i: (i, 0)),
                  pl.BlockSpec((1, d), lambda i: (0, 0)),
                  pl.BlockSpec((1, d), lambda i: (0, 0))],
        out_specs=pl.BlockSpec((tm, d), lambda i: (i, 0)),
        out_shape=jax.ShapeDtypeStruct((n, d), F32),
        compiler_params=_cparams("parallel"),
        name="ln_in",
    )(x, g.reshape(1, d), b.reshape(1, d))


P_Q = 0
P_K = P_Q + ATT_WIDTH
P_V = P_K + 2 * ATT_KV_WIDTH
P_GQ = P_V + ATT_KV_WIDTH
P_GK = P_GQ + GLA_QK_WIDTH
P_GV = P_GK + GLA_QK_WIDTH
P_GR = P_GV + GLA_WIDTH
P_LR = P_GR + GLA_WIDTH
P_FX = P_LR + LANES
P_WIDTH = P_FX + FNET_WIDTH

Q_SCALE = HEAD_DIM ** -0.5 * math.log2(math.e)


def _pad_in_proj(w_in):
    l, d, _ = w_in.shape
    o = 0
    aq = w_in[..., o:o + ATT_WIDTH]; o += ATT_WIDTH
    ak = w_in[..., o:o + ATT_KV_WIDTH]; o += ATT_KV_WIDTH
    av = w_in[..., o:o + ATT_KV_WIDTH]; o += ATT_KV_WIDTH
    gq = w_in[..., o:o + GLA_QK_WIDTH]; o += GLA_QK_WIDTH
    gk = w_in[..., o:o + GLA_QK_WIDTH]; o += GLA_QK_WIDTH
    gv = w_in[..., o:o + GLA_WIDTH]; o += GLA_WIDTH
    gr = w_in[..., o:o + GLA_WIDTH]; o += GLA_WIDTH
    lr = w_in[..., o:o + 2 * GLA_GATE_RANK]; o += 2 * GLA_GATE_RANK
    fx = w_in[..., o:o + FNET_WIDTH]
    k0, k1 = ak[..., :HEAD_DIM], ak[..., HEAD_DIM:]
    zlr = jnp.zeros((l, d, LANES - 2 * GLA_GATE_RANK), w_in.dtype)
    cols = [aq, k0, k0, k1, k1, av, gq, gk, gv, gr, lr, zlr, fx]
    return jnp.concatenate(cols, axis=-1).astype(BF16)


def _proj_kernel(h_ref, w_ref, cos_ref, sin_ref, qg_ref, kg_ref, bd_ref, gm_ref, gb_ref,
                 cc_ref, sc_ref,
                 q_ref, k2_ref, v_ref, gq_ref, gk_ref, gv_ref, gr_ref, la_ref, zc_ref, zs_ref):
    acc = jnp.dot(h_ref[...].astype(BF16), w_ref[...], preferred_element_type=F32)
    cos = cos_ref[...]
    sin = sin_ref[...]
    bd = bd_ref[...]
    lane = lax.broadcasted_iota(jnp.int32, (1, LANES), 1)
    first_half = (lane % (2 * ROPE_FREQS)) < ROPE_FREQS

    def norm_rope(x, gain, scale):
        ss = jnp.dot((x * x).astype(BF16), bd, preferred_element_type=F32)
        xn = x * lax.rsqrt(ss * (1.0 / HEAD_DIM) + RMS_EPS) * gain
        partner = jnp.where(first_half, pltpu.roll(xn, LANES - ROPE_FREQS, 1), pltpu.roll(xn, ROPE_FREQS, 1))
        return (xn * cos + partner * sin) * scale

    for c in range(ATT_WIDTH // LANES):
        x = acc[:, P_Q + c * LANES:P_Q + (c + 1) * LANES]
        q_ref[:, c * LANES:(c + 1) * LANES] = norm_rope(x, qg_ref[...], Q_SCALE).astype(BF16)
    for c in range(2 * ATT_KV_WIDTH // LANES):
        x = acc[:, P_K + c * LANES:P_K + (c + 1) * LANES]
        k2_ref[:, c * LANES:(c + 1) * LANES] = norm_rope(x, kg_ref[...], 1.0).astype(BF16)
    v_ref[...] = acc[:, P_V:P_V + ATT_KV_WIDTH].astype(BF16)

    gq_ref[...] = (acc[:, P_GQ:P_GQ + GLA_QK_WIDTH] * (GLA_DK ** -0.5)).astype(BF16)
    gk_ref[...] = acc[:, P_GK:P_GK + GLA_QK_WIDTH].astype(BF16)
    gv_ref[...] = acc[:, P_GV:P_GV + GLA_WIDTH].astype(BF16)
    gr_ref[...] = acc[:, P_GR:P_GR + GLA_WIDTH].astype(BF16)
    z = jnp.dot(acc[:, P_LR:P_LR + LANES].astype(BF16), gm_ref[...], preferred_element_type=F32) + gb_ref[...]
    la_ref[...] = (jnp.minimum(z, 0.0) - jnp.log(1.0 + jnp.exp(-jnp.abs(z)))) * (1.0 / GLA_TAU)

    fx = acc[:, P_FX:P_FX + FNET_WIDTH].astype(BF16)
    zc_ref[...] = jnp.dot(fx, cc_ref[...], preferred_element_type=F32).astype(BF16)
    zs_ref[...] = jnp.dot(fx, sc_ref[...], preferred_element_type=F32).astype(BF16)


def _proj(h, w, cos, sin, qg, kg, bd, gm, gb, cc, sc, batch, seq, tm):
    n, d = h.shape
    nst = seq // tm
    row = lambda w_: pl.BlockSpec((tm, w_), lambda i: (i, 0))
    const = lambda a: pl.BlockSpec(a.shape, lambda i: (0,) * a.ndim)
    tab = pl.BlockSpec((tm, LANES), lambda i: (i % nst, 0))
    zspec = pl.BlockSpec((tm, FNET_WIDTH), lambda i: (i % nst, i // nst))
    outs = [(ATT_WIDTH, BF16), (2 * ATT_KV_WIDTH, BF16), (ATT_KV_WIDTH, BF16), (GLA_QK_WIDTH, BF16),
            (GLA_QK_WIDTH, BF16), (GLA_WIDTH, BF16), (GLA_WIDTH, BF16), (2 * GLA_QK_WIDTH, F32)]
    return pl.pallas_call(
        _proj_kernel,
        grid=(n // tm,),
        in_specs=[row(d), const(w), tab, tab, const(qg), const(kg), const(bd), const(gm),
                  const(gb), const(cc), const(sc)],
        out_specs=[row(w_) for w_, _ in outs] + [zspec, zspec],
        out_shape=[jax.ShapeDtypeStruct((n, w_), dt) for w_, dt in outs]
        + [jax.ShapeDtypeStruct((seq, batch * FNET_WIDTH), BF16)] * 2,
        compiler_params=_cparams("parallel"),
        name="in_proj",
    )(h, w, cos, sin, qg, kg, bd, gm, gb, cc, sc)


ATT_KEY_CHUNK = 512
ATT_V_ROWS = 128


def _attn_kernel(q_ref, k_ref, vt_ref, o_ref):
    s_len = k_ref.shape[1]
    kc = min(ATT_KEY_CHUNK, s_len)
    nkc = s_len // kc
    lane = lax.broadcasted_iota(jnp.int32, (1, LANES), 1)
    low = lane < HEAD_DIM
    outs = []
    for h in range(ATT_HEADS // ATT_KV_HEADS):
        qp = q_ref[0, :, (h // 2) * LANES:(h // 2 + 1) * LANES]
        qms = [jnp.where(low if h % 2 == 0 else jnp.logical_not(low), qp, jnp.zeros_like(qp))]

        def scores(c):
            kblk = k_ref[0, c * kc:(c + 1) * kc, :]
            return [lax.dot_general(kblk, qm, NT_DIMS, preferred_element_type=F32) for qm in qms]

        m = [None] * len(qms)
        acc = [None] * len(qms)
        s_next = scores(0)
        for c in range(nkc):
            s_cur = s_next
            if c + 1 < nkc:
                s_next = scores(c + 1)
            vt = vt_ref[0, 0, :, c * kc:(c + 1) * kc]
            for i in range(len(qms)):
                mc = jnp.max(s_cur[i], axis=0, keepdims=True)
                m_new = mc if c == 0 else jnp.maximum(m[i], mc)
                pe = jnp.exp2(s_cur[i] - m_new).astype(BF16)
                pv = jnp.dot(vt, pe, preferred_element_type=F32)
                acc[i] = pv if c == 0 else acc[i] * jnp.exp2(m[i] - m_new) + pv
                m[i] = m_new
        outs += [a[0:HEAD_DIM] / a[HEAD_DIM:HEAD_DIM + 1] for a in acc]
    o_ref[0] = jnp.transpose(jnp.concatenate(outs, axis=0)).astype(BF16)


def _attention(q, k2, vt, tq):
    b, s, _ = q.shape
    gw = ATT_WIDTH // ATT_KV_HEADS
    return pl.pallas_call(
        _attn_kernel,
        grid=(b, ATT_KV_HEADS, s // tq),
        in_specs=[pl.BlockSpec((1, tq, gw), lambda bi, g, qi: (bi, qi, g)),
                  pl.BlockSpec((1, s, LANES), lambda bi, g, qi: (bi, 0, g)),
                  pl.BlockSpec((1, 1, ATT_V_ROWS, s), lambda bi, g, qi: (bi, g, 0, 0))],
        out_specs=pl.BlockSpec((1, tq, gw), lambda bi, g, qi: (bi, qi, g)),
        out_shape=jax.ShapeDtypeStruct((b, s, ATT_WIDTH), BF16),
        compiler_params=_cparams("parallel", "parallel", "parallel"),
        name="gqa_attention",
    )(q, k2, vt)


GLA_TILE = 128


def _gla_constants(t):
    nl = int(math.log2(t))
    i = np.arange(t)[:, None]
    m = np.arange(t)[None, :]
    fwd = [m <= i, m > i]
    bwd = [m >= i, m < i]
    mask_f, mask_b = [], []
    for l in range(nl):
        s = t >> l
        half = s // 2
        blk = (i // s) * s
        ref = blk + half - 1
        right = (i - blk) >= half
        wf = np.where(right, (m > ref) & (m <= i), (m > i) & (m <= ref))
        wb = np.where(right, (m > ref) & (m < i), (m >= i) & (m <= ref))
        fwd.append(wf)
        bwd.append(wb)
        same = (i // s) == (m // s)
        mask_f.append(same)
        mask_b.append(same)
    mask_f.append(i == m)
    wst = np.stack([np.concatenate(fwd, 0), np.concatenate(bwd, 0)]).astype(np.float32)
    tile4 = lambda a: np.tile(a.astype(np.float32), (GLA_HEADS, 1))
    mf = np.stack([tile4(a) for a in mask_f])
    mb = np.stack([tile4(a) for a in mask_b])
    return nl, wst, mf, mb


def _gla_kernel(nl, t, q_ref, k_ref, v_ref, la_ref, r_ref, w_ref, mf_ref, mb_ref, bdm_ref, bdn_ref, g_ref,
                o_ref, st_ref, acc_ref):
    s = q_ref.shape[1]
    nt = s // t
    row = lax.broadcasted_iota(jnp.int32, (t, GLA_QK_WIDTH), 0)
    lane_k = lax.broadcasted_iota(jnp.int32, (1, GLA_QK_WIDTH), 1)
    lane_v = lax.broadcasted_iota(jnp.int32, (1, GLA_WIDTH), 1)
    head_k = [(lane_k // GLA_DK) == h for h in range(GLA_HEADS)]
    head_v = [(lane_v // GLA_DV) == h for h in range(GLA_HEADS)]

    dirs = (0, 1)

    def stack_heads(a):
        return jnp.concatenate([jnp.where(head_k[h], a, 0.0) for h in range(GLA_HEADS)], axis=0).astype(BF16)

    def tile_pair(t0s):
        q, k, v, x, o_inter, sc = [], [], [], [], [], []
        for d in dirs:
            t0 = t0s[d]
            q.append(q_ref[0, pl.ds(t0, t), :].astype(F32))
            k.append(k_ref[0, pl.ds(t0, t), :].astype(F32))
            v.append(v_ref[0, pl.ds(t0, t), :])
            la = la_ref[0, pl.ds(t0, t), d * GLA_QK_WIDTH:(d + 1) * GLA_QK_WIDTH]
            la_hi = la.astype(BF16)
            la_lo = (la - la_hi.astype(F32)).astype(BF16)
            wst = w_ref[d]
            x.append(jnp.exp(jnp.dot(wst, la_hi, preferred_element_type=F32)
                             + jnp.dot(wst, la_lo, preferred_element_type=F32)))
        for d in dirs:
            xq = x[d][0:t]
            q_st = (q[d] * xq).astype(BF16)
            k_st = (k[d] * x[d][t:2 * t]).astype(BF16)
            dec = xq[t - 1:t] if d == 0 else xq[0:1]
            st = st_ref[d]
            o_inter.append(lax.dot_general(q_st, st.astype(BF16), NT_DIMS, preferred_element_type=F32))
            kv = lax.dot_general(v[d], k_st, TN_DIMS, preferred_element_type=F32)
            st_ref[d] = st * dec + kv * bdm_ref[...]
            sc.append(jnp.zeros((GLA_HEADS * t, t), F32))
        for l in range(nl):
            half = t >> (l + 1)
            right = ((row // half) % 2) == 1
            for d in dirs:
                qside = right if d == 0 else jnp.logical_not(right)
                g = jnp.where(qside, q[d], k[d]) * x[d][(2 + l) * t:(3 + l) * t]
                a4 = stack_heads(jnp.where(qside, g, 0.0))
                b = jnp.where(qside, 0.0, g).astype(BF16)
                m_ref = mf_ref if d == 0 else mb_ref
                sc[d] = sc[d] + lax.dot_general(a4, b, NT_DIMS, preferred_element_type=F32) * m_ref[l]
        sc[0] = sc[0] + lax.dot_general(stack_heads(q[0]), k[0].astype(BF16), NT_DIMS,
                                        preferred_element_type=F32) * mf_ref[nl]
        outs = []
        for d in dirs:
            o4 = jnp.dot(sc[d].astype(BF16), v[d], preferred_element_type=F32)
            o = o_inter[d]
            for h in range(GLA_HEADS):
                o = o + jnp.where(head_v[h], o4[h * t:(h + 1) * t], 0.0)
            outs.append(o)
        return outs

    def finish(t0, tot):
        ss = jnp.dot((tot * tot).astype(BF16), bdn_ref[...], preferred_element_type=F32)
        r = r_ref[0, pl.ds(t0, t), :].astype(F32)
        y = tot * lax.rsqrt(ss * (1.0 / GLA_DV) + RMS_EPS) * g_ref[...] * (r / (1.0 + jnp.exp(-r)))
        o_ref[0, pl.ds(t0, t), :] = y.astype(BF16)

    def tiles(i):
        return pl.multiple_of(i * t, t), pl.multiple_of((nt - 1 - i) * t, t)

    st_ref[...] = jnp.zeros_like(st_ref)

    def park(i, c):
        t0s = tiles(i)
        for t0, o in zip(t0s, tile_pair(t0s)):
            acc_ref[pl.ds(t0, t), :] = o
        return c

    def complete(i, c):
        t0s = tiles(i)
        for t0, o in zip(t0s, tile_pair(t0s)):
            finish(t0, acc_ref[pl.ds(t0, t), :] + o)
        return c

    lax.fori_loop(0, nt // 2, park, 0)
    lax.fori_loop(nt // 2, nt, complete, 0)


def _gla(gq, gk, gv, la, gr, wst, mf, mb, bdm, bdn, g, nl, t):
    b, s, _ = gq.shape
    seq = lambda w_: pl.BlockSpec((1, s, w_), lambda bi: (bi, 0, 0))
    const = lambda a: pl.BlockSpec(a.shape, lambda bi: (0,) * a.ndim)
    return pl.pallas_call(
        functools.partial(_gla_kernel, nl, t),
        grid=(b,),
        in_specs=[seq(GLA_QK_WIDTH), seq(GLA_QK_WIDTH), seq(GLA_WIDTH), seq(2 * GLA_QK_WIDTH), seq(GLA_WIDTH),
                  const(wst), const(mf), const(mb), const(bdm), const(bdn), const(g)],
        out_specs=seq(GLA_WIDTH),
        out_shape=jax.ShapeDtypeStruct((b, s, GLA_WIDTH), BF16),
        scratch_shapes=[pltpu.VMEM((2, GLA_WIDTH, GLA_QK_WIDTH), F32), pltpu.VMEM((s, GLA_WIDTH), F32)],
        compiler_params=_cparams("parallel"),
        name="gla_scan",
    )(gq, gk, gv, la, gr, wst, mf, mb, bdm, bdn, g)


def _dft_kernel(fc_ref, fs_ref, zc_ref, zs_ref, o_ref):
    o_ref[...] = (jnp.dot(fc_ref[...], zc_ref[...], preferred_element_type=F32)
                  + jnp.dot(fs_ref[...], zs_ref[...], preferred_element_type=F32)).astype(BF16)


def _position_dft(fc, fs, zc, zs, tm, tn):
    s, w = zc.shape
    return pl.pallas_call(
        _dft_kernel,
        grid=(w // tn, s // tm),
        in_specs=[pl.BlockSpec((tm, s), lambda j, i: (i, 0)),
                  pl.BlockSpec((tm, s), lambda j, i: (i, 0)),
                  pl.BlockSpec((s, tn), lambda j, i: (0, j)),
                  pl.BlockSpec((s, tn), lambda j, i: (0, j))],
        out_specs=pl.BlockSpec((tm, tn), lambda j, i: (i, j)),
        out_shape=jax.ShapeDtypeStruct((s, w), BF16),
        compiler_params=_cparams("parallel", "parallel"),
        name="position_dft",
    )(fc, fs, zc, zs)


def _outproj_kernel(alpha, n_exp, oatt_ref, ogla_ref, offt_ref, h_ref, wo_ref, g_ref, b_ref,
                    wrh_ref, wrl_ref, rb_ref, us_ref,
                    h1_ref, idx_ref, wts_ref, rank_ref, cnt_ref, carry_ref):
    i = pl.program_id(0)

    @pl.when(i == 0)
    def _():
        carry_ref[...] = jnp.zeros_like(carry_ref)

    acc = jnp.dot(oatt_ref[...], wo_ref[0:ATT_WIDTH, :], preferred_element_type=F32)
    acc = acc + jnp.dot(ogla_ref[...], wo_ref[ATT_WIDTH:ATT_WIDTH + GLA_WIDTH, :], preferred_element_type=F32)
    acc = acc + jnp.dot(offt_ref[...], wo_ref[ATT_WIDTH + GLA_WIDTH:, :], preferred_element_type=F32)
    h1 = _layer_norm_rows(alpha * h_ref[...] + acc, g_ref[...], b_ref[...])
    h1_ref[...] = h1

    hh = h1.astype(BF16)
    hl = (h1 - hh.astype(F32)).astype(BF16)
    wrh = wrh_ref[...]
    by_hh = lax.dot_general(jnp.concatenate([wrh, wrl_ref[...]], axis=0), hh, NT_DIMS, preferred_element_type=F32)
    logits = (by_hh[:n_exp] + by_hh[n_exp:]
              + lax.dot_general(wrh, hl, NT_DIMS, preferred_element_type=F32)) + rb_ref[...]
    tm = logits.shape[1]
    e_iota = lax.broadcasted_iota(jnp.int32, (n_exp, tm), 0)
    cur = logits
    vals, idxs, sels = [], [], []
    for _k in range(TOP_K):
        m = jnp.max(cur, axis=0, keepdims=True)
        ik = jnp.min(jnp.where(cur == m, e_iota, n_exp), axis=0, keepdims=True)
        sel = e_iota == ik
        vals.append(m)
        idxs.append(ik)
        sels.append(sel)
        cur = jnp.where(sel, -jnp.inf, cur)
    ex = [jnp.exp(v - vals[0]) for v in vals]
    den = ex[0] + ex[1] + ex[2] + ex[3]
    zero = jnp.zeros_like(den)
    idx_ref[...] = jnp.concatenate(idxs, axis=0)
    wts_ref[...] = jnp.concatenate([e / den for e in ex] + [zero] * (wts_ref.shape[0] - TOP_K), axis=0)

    onehot = jnp.zeros((n_exp, tm), F32)
    for sel in sels:
        onehot = onehot + sel.astype(F32)
    before = jnp.dot(onehot.astype(BF16), us_ref[...], preferred_element_type=F32) + carry_ref[:, 0:1]
    ranks = [jnp.sum(jnp.where(sel, before, 0.0), axis=0, keepdims=True) for sel in sels]
    rank_ref[...] = jnp.concatenate(ranks, axis=0).astype(jnp.int32)
    carry = carry_ref[...] + jnp.sum(onehot, axis=1, keepdims=True)
    carry_ref[...] = carry
    cnt_ref[...] = carry.astype(jnp.int32)


def _outproj(alpha, o_att, o_gla, o_fft, h, wo, g, b, wrh, wrl, rb, us, seq, tm):
    n, d = h.shape
    n_exp = wrh.shape[0]
    nst = seq // tm
    row = lambda w_: pl.BlockSpec((tm, w_), lambda i: (i, 0))
    const = lambda a: pl.BlockSpec(a.shape, lambda i: (0,) * a.ndim)
    tok = lambda r: pl.BlockSpec((r, tm), lambda i: (0, i))
    return pl.pallas_call(
        functools.partial(_outproj_kernel, alpha, n_exp),
        grid=(n // tm,),
        in_specs=[row(ATT_WIDTH), row(GLA_WIDTH),
                  pl.BlockSpec((tm, FNET_WIDTH), lambda i: (i % nst, i // nst)),
                  row(d), const(wo), const(g), const(b), const(wrh), const(wrl), const(rb), const(us)],
        out_specs=[row(d), tok(TOP_K), tok(2 * TOP_K), tok(TOP_K),
                   pl.BlockSpec((n_exp, LANES), lambda i: (0, 0))],
        out_shape=[jax.ShapeDtypeStruct((n, d), F32),
                   jax.ShapeDtypeStruct((TOP_K, n), jnp.int32),
                   jax.ShapeDtypeStruct((2 * TOP_K, n), F32),
                   jax.ShapeDtypeStruct((TOP_K, n), jnp.int32),
                   jax.ShapeDtypeStruct((n_exp, LANES), jnp.int32)],
        scratch_shapes=[pltpu.VMEM((n_exp, LANES), F32)],
        compiler_params=_cparams("arbitrary"),
        name="out_proj_router",
    )(o_att, o_gla, o_fft, h, wo, g, b, wrh, wrl, rb, us)


DMA_UNROLL = 8
ROW_SUB = 8


def _row_copy(src, dst, sem):
    return pltpu.make_async_copy(src, dst, sem)


def _dispatch_kernel(tb, nb, dest_ref, h_ref, xs_in_ref, xs_ref, buf, sem):
    del xs_in_ref
    i = pl.program_id(0)
    slot = i % 2
    buf[slot] = h_ref[...].reshape(buf.shape[1:])

    def start(t, c):
        for k in range(TOP_K):
            _row_copy(buf.at[slot, t], xs_ref.at[dest_ref[k, t]], sem.at[slot]).start(priority=k % 2)
        return c

    lax.fori_loop(0, tb, start, 0, unroll=DMA_UNROLL)

    def wait_block(s):
        def wait(t, c):
            for k in range(TOP_K):
                _row_copy(buf.at[s, 0], xs_ref.at[0], sem.at[s]).wait()
            return c
        lax.fori_loop(0, tb, wait, 0, unroll=DMA_UNROLL)

    @pl.when(i > 0)
    def _():
        wait_block(1 - slot)

    @pl.when(i == nb - 1)
    def _():
        wait_block(slot)


def _dispatch(dest, h, xs_prev, tb):
    n, d = h.shape
    return pl.pallas_call(
        functools.partial(_dispatch_kernel, tb, n // tb),
        grid=(n // tb,),
        in_specs=[pl.BlockSpec((TOP_K, tb), lambda i: (0, i), memory_space=pltpu.SMEM),
                  pl.BlockSpec((tb, d), lambda i: (i, 0)),
                  pl.BlockSpec(memory_space=pl.ANY)],
        out_specs=pl.BlockSpec(memory_space=pl.ANY),
        out_shape=jax.ShapeDtypeStruct(xs_prev.shape, F32),
        scratch_shapes=[pltpu.VMEM((2, tb, ROW_SUB, d // ROW_SUB), F32), pltpu.SemaphoreType.DMA((2,))],
        input_output_aliases={2: 0},
        compiler_params=_cparams("arbitrary"),
        name="moe_dispatch",
    )(dest, h, xs_prev)


def _ffn_kernel(te_ref, tv_ref, tof_ref, x_ref, wgu_ref, bgu_ref, wd_ref, bd_ref, y_ref, wgu_bf, wd_bf):
    j = pl.program_id(0)
    valid = tv_ref[j]
    new_expert = jnp.logical_or(j == 0, te_ref[j] != te_ref[jnp.maximum(j - 1, 0)])

    @pl.when(jnp.logical_and(new_expert, valid > 0))
    def _():
        wgu_bf[...] = wgu_ref[0, 0].astype(BF16)
        wd_bf[...] = wd_ref[0, 0].astype(BF16)

    @pl.when(valid > 0)
    def _():
        de, d = wd_bf.shape
        x = x_ref[...].reshape(x_ref.shape[0], d).astype(BF16)
        gu = jnp.dot(x, wgu_bf[...], preferred_element_type=F32) + bgu_ref[0, 0]
        gate = jnp.minimum(gu[:, :de], SWIGLU_LIMIT)
        up = jnp.clip(gu[:, de:], -SWIGLU_LIMIT, SWIGLU_LIMIT)
        hid = (up + 1.0) * (gate / (1.0 + jnp.exp(-SWIGLU_ALPHA * gate)))
        y = jnp.dot(hid.astype(BF16), wd_bf[...], preferred_element_type=F32) + bd_ref[0, 0]
        y_ref[...] = y.reshape(y_ref.shape)

    @pl.when(valid == 0)
    def _():
        y_ref[...] = jnp.zeros_like(y_ref)


def _expert_ffn(layer, tile_expert, tile_valid, tile_of, xs, wgu, bgu, wd, bd, tmf):
    rows = xs.shape[0]
    depth, n_exp, d, de2 = wgu.shape
    de = de2 // 2
    xspec = pl.BlockSpec((tmf,) + xs.shape[1:], lambda j, te, tv, tof: (tof[j], 0, 0))
    yspec = pl.BlockSpec((tmf,) + xs.shape[1:], lambda j, te, tv, tof: (j, 0, 0))
    wspec = lambda r, c: pl.BlockSpec((1, 1, r, c), lambda j, te, tv, tof: (layer, te[j], 0, 0))
    grid_spec = pltpu.PrefetchScalarGridSpec(
        num_scalar_prefetch=3,
        grid=(rows // tmf,),
        in_specs=[xspec, wspec(d, de2), wspec(1, de2), wspec(de, d), wspec(1, d)],
        out_specs=yspec,
        scratch_shapes=[pltpu.VMEM((d, de2), BF16), pltpu.VMEM((de, d), BF16)],
    )
    return pl.pallas_call(
        _ffn_kernel,
        grid_spec=grid_spec,
        out_shape=jax.ShapeDtypeStruct(xs.shape, F32),
        compiler_params=_cparams("arbitrary"),
        name="moe_expert_ffn",
    )(tile_expert, tile_valid, tile_of, xs, wgu, bgu.reshape(depth, n_exp, 1, de2), wd,
      bd.reshape(depth, n_exp, 1, d))


def _combine_kernel(alpha, tb, nb, dcur_ref, dnxt_ref, w_ref, h1_ref, g_ref, b_ref, y_ref, o_ref, ybuf, sem):
    i = pl.program_id(0)
    slot = i % 2
    nslot = 1 - slot
    d = h1_ref.shape[1]

    def start_block(dref, s):
        def start(t, c):
            for k in range(TOP_K):
                _row_copy(y_ref.at[dref[k, t]], ybuf.at[s, k, t], sem.at[s]).start(priority=k % 2)
            return c
        lax.fori_loop(0, tb, start, 0, unroll=DMA_UNROLL)

    def wait_block(s):
        def wait(t, c):
            for k in range(TOP_K):
                _row_copy(y_ref.at[0], ybuf.at[s, k, 0], sem.at[s]).wait()
            return c
        lax.fori_loop(0, tb, wait, 0, unroll=DMA_UNROLL)

    @pl.when(i == 0)
    def _():
        start_block(dcur_ref, slot)

    @pl.when(i + 1 < nb)
    def _():
        start_block(dnxt_ref, nslot)

    wait_block(slot)
    wt = jnp.transpose(w_ref[...])
    acc = alpha * h1_ref[...]
    for k in range(TOP_K):
        acc = acc + ybuf[slot, k].reshape(tb, d) * wt[:, k:k + 1]
    o_ref[...] = _layer_norm_rows(acc, g_ref[...], b_ref[...])


def _combine(alpha, dest, wts, h1, g, b, ys, tb):
    n, d = h1.shape
    nb = n // tb
    dspec = lambda f: pl.BlockSpec((TOP_K, tb), f, memory_space=pltpu.SMEM)
    return pl.pallas_call(
        functools.partial(_combine_kernel, alpha, tb, nb),
        grid=(nb,),
        in_specs=[dspec(lambda i: (0, i)),
                  dspec(lambda i: (0, jnp.minimum(i + 1, nb - 1))),
                  pl.BlockSpec((2 * TOP_K, tb), lambda i: (0, i)),
                  pl.BlockSpec((tb, d), lambda i: (i, 0)),
                  pl.BlockSpec((1, d), lambda i: (0, 0)),
                  pl.BlockSpec((1, d), lambda i: (0, 0)),
                  pl.BlockSpec(memory_space=pl.ANY)],
        out_specs=pl.BlockSpec((tb, d), lambda i: (i, 0)),
        out_shape=jax.ShapeDtypeStruct((n, d), F32),
        scratch_shapes=[pltpu.VMEM((2, TOP_K, tb) + ys.shape[1:], F32),
                        pltpu.SemaphoreType.DMA((2,))],
        compiler_params=_cparams("arbitrary"),
        name="moe_combine",
    )(dest, dest, wts, h1, g.reshape(1, d), b.reshape(1, d), ys)


def _rope_tables(seq):
    pos = jnp.arange(seq)
    row_id = (pos // GRID_W).astype(F32)
    col_id = (pos % GRID_W).astype(F32)
    inv_freq = 1.0 / (ROPE_THETA ** (jnp.arange(ROPE_FREQS, dtype=F32) / ROPE_FREQS))
    lane = np.arange(LANES)
    hd = lane % HEAD_DIM
    freq = hd % ROPE_FREQS
    use_col = (hd // (2 * ROPE_FREQS)) == 1
    second = (hd % (2 * ROPE_FREQS)) >= ROPE_FREQS
    ang = jnp.where(use_col[None, :], col_id[:, None], row_id[:, None]) * inv_freq[freq][None, :]
    sign = np.where(second, 1.0, -1.0).astype(np.float32)
    return jnp.cos(ang), jnp.sin(ang) * sign[None, :]


def _dft_tables(seq):
    a = _pick(FNET_GROUP_DIM, seq)
    t = jnp.arange(seq, dtype=jnp.int32)

    def cos_sin(rows):
        ang = ((rows[:, None] * t[None, :]) % seq).astype(F32) * (2.0 * math.pi / seq)
        return jnp.cos(ang), jnp.sin(ang)

    cq, sq = [v[:, None, :] for v in cos_sin(jnp.arange(seq // a, dtype=jnp.int32) * a)]
    cr, sr = [v[None, :, :] for v in cos_sin(jnp.arange(a, dtype=jnp.int32))]
    scale = seq ** -0.5
    fc = ((cq * cr - sq * sr) * scale).reshape(seq, seq).astype(BF16)
    fs = ((sq * cr + cq * sr) * -scale).reshape(seq, seq).astype(BF16)
    c = np.arange(FNET_GROUP_DIM)
    angc = 2.0 * np.pi * ((c[:, None] * c[None, :]) % FNET_GROUP_DIM) / FNET_GROUP_DIM
    eye = np.eye(FNET_GROUPS)
    cc = np.kron(eye, np.cos(angc)) * FNET_GROUP_DIM ** -0.5
    sc = np.kron(eye, np.sin(angc)) * FNET_GROUP_DIM ** -0.5
    return fc, fs, jnp.asarray(cc, BF16), jnp.asarray(sc, BF16)


def _block_ones(width, group):
    g = np.arange(width) // group
    return jnp.asarray((g[:, None] == g[None, :]).astype(np.float32), BF16)


def _pick(limit, n):
    t = min(limit, n)
    while n % t:
        t //= 2
    return t


def kernel(x, ln_in_g, ln_in_b, w_in, att_q_gain, att_k_gain, gla_gate_w, gla_gate_b, gla_norm_g, w_out,
           ln1_g, ln1_b, router_w, router_b, exp_w_gu, exp_b_gu, exp_w_down, exp_b_down, ln2_g, ln2_b):
    batch, seq, d = x.shape
    n = batch * seq
    depth = w_in.shape[0]
    n_exp = router_w.shape[-1]
    alpha = (2.0 * depth) ** 0.25

    tm = _pick(512, seq)
    tmi = _pick(1024, seq)
    tq = _pick(2048, seq)
    tg = _pick(GLA_TILE, seq // 2)
    tmf = _pick(1024, n)
    tb = _pick(256, n)
    n_tiles = (TOP_K * n) // tmf + n_exp
    rows = n_tiles * tmf

    cos, sin = _rope_tables(seq)
    fc, fs, cc, sc = _dft_tables(seq)
    bd_head = _block_ones(LANES, HEAD_DIM)
    bd_gla = _block_ones(GLA_WIDTH, GLA_DV)
    vt_pad = jnp.zeros((batch, ATT_KV_HEADS, ATT_V_ROWS - HEAD_DIM, seq), BF16).at[:, :, 0, :].set(1.0)
    nl, wst, mf, mb = _gla_constants(tg)
    wst = jnp.asarray(wst, BF16)
    mf = jnp.asarray(mf)
    mb = jnp.asarray(mb)
    hv = np.arange(GLA_WIDTH) // GLA_DV
    hk = np.arange(GLA_QK_WIDTH) // GLA_DK
    bdm = jnp.asarray((hv[:, None] == hk[None, :]).astype(np.float32))
    us = np.arange(tm)
    us = jnp.asarray((us[:, None] < us[None, :]).astype(np.float32), BF16)

    w_pad = _pad_in_proj(w_in)
    qg = jnp.tile(att_q_gain, (1, LANES // HEAD_DIM)).reshape(depth, 1, LANES)
    kg = jnp.tile(att_k_gain, (1, LANES // HEAD_DIM)).reshape(depth, 1, LANES)
    gm = jnp.zeros((depth, LANES, 2 * GLA_QK_WIDTH), F32)
    gm = gm.at[:, :GLA_GATE_RANK, :GLA_QK_WIDTH].set(gla_gate_w[:, 0])
    gm = gm.at[:, GLA_GATE_RANK:2 * GLA_GATE_RANK, GLA_QK_WIDTH:].set(gla_gate_w[:, 1]).astype(BF16)
    gb = gla_gate_b.reshape(depth, 1, 2 * GLA_QK_WIDTH)
    gn = jnp.tile(gla_norm_g, (1, GLA_HEADS)).reshape(depth, 1, GLA_WIDTH)
    wo = w_out.astype(BF16)
    wr_t = jnp.swapaxes(router_w, 1, 2)
    wrh = wr_t.astype(BF16)
    wrl = (wr_t - wrh.astype(F32)).astype(BF16)
    rb = router_b.reshape(depth, n_exp, 1)

    e_ids = jnp.arange(n_exp, dtype=jnp.int32)
    tile_ids = jnp.arange(n_tiles, dtype=jnp.int32)
    xs = jnp.zeros((rows, ROW_SUB, d // ROW_SUB), F32)

    h = _layer_norm(x.reshape(n, d), ln_in_g, ln_in_b, tm)
    for l in range(depth):
        q, k2, av, gq, gk, gv, gr, la, zc, zs = _proj(
            h, w_pad[l], cos, sin, qg[l], kg[l], bd_head, gm[l], gb[l], cc, sc, batch, seq, tmi)
        vt = jnp.transpose(av.reshape(batch, seq, ATT_KV_HEADS, HEAD_DIM), (0, 2, 3, 1))
        vt = jnp.concatenate([vt, vt_pad], axis=2)
        o_att = _attention(q.reshape(batch, seq, -1), k2.reshape(batch, seq, -1), vt, tq)
        o_gla = _gla(gq.reshape(batch, seq, -1), gk.reshape(batch, seq, -1), gv.reshape(batch, seq, -1),
                     la.reshape(batch, seq, -1), gr.reshape(batch, seq, -1), wst, mf, mb, bdm, bd_gla, gn[l], nl, tg)
        o_fft = _position_dft(fc, fs, zc, zs, tm, _pick(512, batch * FNET_WIDTH))
        h1, idx, wts, rank, cnt = _outproj(
            alpha, o_att.reshape(n, -1), o_gla.reshape(n, -1), o_fft, h, wo[l], ln1_g[l].reshape(1, d),
            ln1_b[l].reshape(1, d), wrh[l], wrl[l], rb[l], us, seq, tm)

        counts = cnt[:, 0]
        tiles_per = (counts + tmf - 1) // tmf
        tile_end = jnp.cumsum(tiles_per)
        offs = (tile_end - tiles_per) * tmf
        chosen = idx[:, :, None] == e_ids[None, None, :]
        dest = rank + jnp.sum(jnp.where(chosen, offs[None, None, :], 0), axis=-1)
        used = tile_end[-1]
        tile_of = jnp.minimum(tile_ids, used - 1)
        tile_expert = jnp.sum((tile_end[None, :] <= tile_of[:, None]).astype(jnp.int32), axis=1)
        own = tile_expert[:, None] == e_ids[None, :]
        group_end = jnp.sum(jnp.where(own, (offs + counts)[None, :], 0), axis=1)
        tile_valid = jnp.where(tile_ids < used, jnp.clip(group_end - tile_ids * tmf, 0, tmf), 0).astype(jnp.int32)

        xs = _dispatch(dest, h1, xs, tb)
        ys = _expert_ffn(l, tile_expert, tile_valid, tile_of, xs, exp_w_gu, exp_b_gu, exp_w_down, exp_b_down, tmf)
        h = _combine(alpha, dest, wts, h1, ln2_g[l], ln2_b[l], ys, tb)
    return h.reshape(batch, seq, d)
```
